```python
import jax, jax.numpy as jnp
from jax import lax
import numpy as np

D_MODEL = 1024
BATCH = 32
SEQ = 256
DEPTH = 4
DEC_BATCH = 8
DEC_SEQ = 2048
PAST_LEN = 512

GRID_W = 64
N_MIXERS = 4
N_REPEAT = DEPTH // N_MIXERS
HEAD_DIM = 64
N_HEADS = D_MODEL // HEAD_DIM
N_KV_HEADS = N_HEADS // 4
Q_BLOCK = 128
NA_WIN_ROWS = 8
NA_WIN_COLS = 16
NA_QCOLS = 16
NA_KCOLS = 32
SWA_WINDOW = 128
MLA_Q_LORA = 256
MLA_KV_LORA = 128
MLA_NOPE = 64
MLA_ROPE = 32
MLA_V = 64
FFN_HIDDEN = ((8 * D_MODEL + 3 * 256 - 1) // (3 * 256)) * 256
ROPE_THETA = 10000.0
EPS = 1e-6
NEG_INF = -1e30

kernel_name = 'hybrid_dit_prefix_context_step'


def rms_norm(x, g):
    xf = x.astype(jnp.float32)
    y = xf * lax.rsqrt(jnp.mean(xf * xf, axis=-1, keepdims=True) + EPS)
    return (y * g.astype(jnp.float32)).astype(x.dtype)


def adaln(cond, w_mod, b_mod):
    m = jax.nn.silu(cond) @ w_mod + b_mod
    return [t[:, None, :] for t in jnp.split(m, 6, axis=-1)]


def modulate(h, shift, scale):
    return h * (1 + scale) + shift


def swiglu(h, w_gate, w_up, w_down):
    return (jax.nn.silu(h @ w_gate) * (h @ w_up)) @ w_down


def grid_positions(n):
    t = jnp.arange(n, dtype=jnp.int32)
    return t // GRID_W, t % GRID_W


def axial_rope(x, rows, cols):
    half = x.shape[-1] // 2
    quarter = half // 2
    inv_freq = ROPE_THETA ** (-jnp.arange(quarter, dtype=jnp.float32) / quarter)

    def rotate(xh, pos):
        ang = pos.astype(jnp.float32)[:, None] * inv_freq[None, :]
        cos, sin = jnp.cos(ang)[:, None, :], jnp.sin(ang)[:, None, :]
        x1 = xh[..., :quarter].astype(jnp.float32)
        x2 = xh[..., quarter:].astype(jnp.float32)
        return jnp.concatenate([x1 * cos - x2 * sin, x2 * cos + x1 * sin], axis=-1)

    out = jnp.concatenate([rotate(x[..., :half], rows), rotate(x[..., half:], cols)], axis=-1)
    return out.astype(x.dtype)


def joint_softmax(parts, sink=None):
    s = jnp.concatenate(parts, axis=-1) if len(parts) > 1 else parts[0]
    m = jnp.max(s, axis=-1, keepdims=True)
    if sink is not None:
        m = jnp.maximum(m, sink)
    p = jnp.exp(s - m)
    den = jnp.sum(p, axis=-1, keepdims=True)
    if sink is not None:
        den = den + jnp.exp(sink - m)
    p = p / den
    out, off = [], 0
    for t in parts:
        n = t.shape[-1]
        out.append(p[..., off:off + n])
        off += n
    return out


def block_attention(q, k, v, sink=None):
    B, Lq, Hq, dk = q.shape
    Hkv, dv = k.shape[2], v.shape[-1]
    G = Hq // Hkv
    nb = Lq // Q_BLOCK
    scale = dk ** -0.5
    qb = jnp.moveaxis(q.reshape(B, nb, Q_BLOCK, Hkv, G, dk), 1, 0)
    sk = None if sink is None else sink.astype(jnp.float32).reshape(1, Hkv, G, 1, 1)

    def one_block(qi):
        s = jnp.einsum('bqhgd,bkhd->bhgqk', qi, k).astype(jnp.float32) * scale
        (p,) = joint_softmax([s], sk)
        return jnp.einsum('bhgqk,bkhd->bqhgd', p.astype(v.dtype), v)

    o = lax.map(one_block, qb)
    return jnp.moveaxis(o, 0, 1).reshape(B, Lq, Hq, dv)


def na_latent_attention(q, k, v, kc, vc, rpb):
    B, L, H, dh = q.shape
    rows = L // GRID_W
    wr = min(NA_WIN_ROWS, rows)
    nqb = GRID_W // NA_QCOLS
    scale = dh ** -0.5
    qcol = np.arange(GRID_W)
    col_start = np.clip(qcol - NA_WIN_COLS // 2, 0, GRID_W - NA_WIN_COLS)
    kc0 = np.minimum(col_start[::NA_QCOLS], GRID_W - NA_KCOLS)
    kcol = kc0[:, None] + np.arange(NA_KCOLS)[None, :]
    cs = col_start.reshape(nqb, NA_QCOLS)[:, :, None]
    col_valid = (kcol[:, None, :] >= cs) & (kcol[:, None, :] < cs + NA_WIN_COLS)
    dcol_idx = np.clip(kcol[:, None, :] - qcol.reshape(nqb, NA_QCOLS)[:, :, None] + NA_WIN_COLS - 1,
                       0, 2 * NA_WIN_COLS - 2)
    kg = k.reshape(B, rows, GRID_W, H, dh)
    vg = v.reshape(B, rows, GRID_W, H, dh)
    q_rows = jnp.moveaxis(q.reshape(B, rows, nqb, NA_QCOLS, H, dh), 1, 0)
    n_loc = wr * NA_KCOLS

    def one_row(args):
        r, qr = args
        rs = jnp.clip(r - wr // 2, 0, rows - wr)
        kb = lax.dynamic_slice_in_dim(kg, rs, wr, axis=1)[:, :, kcol]
        vb = lax.dynamic_slice_in_dim(vg, rs, wr, axis=1)[:, :, kcol]
        dr = rs + jnp.arange(wr) - r + (NA_WIN_ROWS - 1)
        bias = rpb[:, dr][:, :, dcol_idx].transpose(0, 2, 3, 1, 4)
        s_loc = jnp.einsum('bjqhd,brjkhd->bhjqrk', qr, kb).astype(jnp.float32) * scale
        s_loc = jnp.where(col_valid[:, :, None, :], s_loc + bias.astype(jnp.float32), NEG_INF)
        s_loc = s_loc.reshape(B, H, nqb, NA_QCOLS, n_loc)
        s_ctx = jnp.einsum('bjqhd,bchd->bhjqc', qr, kc).astype(jnp.float32) * scale
        p_loc, p_ctx = joint_softmax([s_loc, s_ctx])
        p_loc = p_loc.reshape(B, H, nqb, NA_QCOLS, wr, NA_KCOLS).astype(v.dtype)
        return (jnp.einsum('bhjqrk,brjkhd->bjqhd', p_loc, vb)
                + jnp.einsum('bhjqc,bchd->bjqhd', p_ctx.astype(v.dtype), vc))

    o = lax.map(one_row, (jnp.arange(rows), q_rows))
    return jnp.moveaxis(o, 0, 1).reshape(B, L, H, dh)


def swa_latent_attention(q, k, v, kc, vc, sink):
    B, L, Hq, dh = q.shape
    Hkv = k.shape[2]
    G = Hq // Hkv
    nb = L // Q_BLOCK
    span = 3 * Q_BLOCK
    scale = dh ** -0.5
    pad = ((0, 0), (Q_BLOCK, Q_BLOCK), (0, 0), (0, 0))
    kp, vp = jnp.pad(k, pad), jnp.pad(v, pad)
    qb = jnp.moveaxis(q.reshape(B, nb, Q_BLOCK, Hkv, G, dh), 1, 0)
    sk = sink.astype(jnp.float32).reshape(1, Hkv, G, 1, 1)

    def one_block(args):
        j, qj = args
        start = j * Q_BLOCK
        kj = lax.dynamic_slice_in_dim(kp, start, span, axis=1)
        vj = lax.dynamic_slice_in_dim(vp, start, span, axis=1)
        kpos = start - Q_BLOCK + jnp.arange(span)
        qpos = start + jnp.arange(Q_BLOCK)
        valid = ((kpos >= 0) & (kpos < L))[None, :] & (jnp.abs(kpos[None, :] - qpos[:, None]) <= SWA_WINDOW)
        s_loc = jnp.einsum('bqhgd,bkhd->bhgqk', qj, kj).astype(jnp.float32) * scale
        s_loc = jnp.where(valid, s_loc, NEG_INF)
        s_ctx = jnp.einsum('bqhgd,bchd->bhgqc', qj, kc).astype(jnp.float32) * scale
        p_loc, p_ctx = joint_softmax([s_loc, s_ctx], sk)
        return (jnp.einsum('bhgqk,bkhd->bqhgd', p_loc.astype(v.dtype), vj)
                + jnp.einsum('bhgqc,bchd->bqhgd', p_ctx.astype(v.dtype), vc))

    o = lax.map(one_block, (jnp.arange(nb), qb))
    return jnp.moveaxis(o, 0, 1).reshape(B, L, Hq, dh)


def mha_project(h, w_qkv, q_g, k_g, n_kv):
    B, L, _ = h.shape
    nq, nk = N_HEADS * HEAD_DIM, n_kv * HEAD_DIM
    qkv = h @ w_qkv
    q = qkv[..., :nq].reshape(B, L, N_HEADS, HEAD_DIM)
    k = qkv[..., nq:nq + nk].reshape(B, L, n_kv, HEAD_DIM)
    v = qkv[..., nq + nk:].reshape(B, L, n_kv, HEAD_DIM)
    return rms_norm(q, q_g), rms_norm(k, k_g), v


def mla_queries(h, w_dq, q_lora_g, w_uq, q_nope_g, q_pe_g):
    B, L, _ = h.shape
    q = (rms_norm(h @ w_dq, q_lora_g) @ w_uq).reshape(B, L, N_HEADS, MLA_NOPE + MLA_ROPE)
    return rms_norm(q[..., :MLA_NOPE], q_nope_g), rms_norm(q[..., MLA_NOPE:], q_pe_g)


def mla_compress(h, w_dkv, kv_lora_g, k_pe_g):
    ckv = h @ w_dkv
    return rms_norm(ckv[..., :MLA_KV_LORA], kv_lora_g), rms_norm(ckv[..., MLA_KV_LORA:], k_pe_g)


def mla_expand(c_kv, k_pe, w_ukv, k_nope_g):
    B, L, _ = c_kv.shape
    kv = (c_kv @ w_ukv).reshape(B, L, N_HEADS, MLA_NOPE + MLA_V)
    k_nope = rms_norm(kv[..., :MLA_NOPE], k_nope_g)
    k = jnp.concatenate([k_nope, jnp.broadcast_to(k_pe, (B, L, N_HEADS, MLA_ROPE)).astype(k_nope.dtype)], axis=-1)
    return k, kv[..., MLA_NOPE:]


def heads_out(o, w_o):
    return o.reshape(o.shape[0], o.shape[1], -1) @ w_o


def setup_inputs(seed: int = 0) -> dict:
    key = jax.random.key(seed)
    keys = iter(jax.random.split(key, 64))

    def normal(shape, scale):
        return jax.random.normal(next(keys), shape, dtype=jnp.float32) * scale

    def gain(shape):
        return 1.0 + normal(shape, 0.05)

    D, R = D_MODEL, N_REPEAT
    hq, hkv = N_HEADS * HEAD_DIM, N_KV_HEADS * HEAD_DIM
    return {
        'x_prompt': normal((BATCH, SEQ, D), 1.0),
        'x_sample': normal((DEC_BATCH, DEC_SEQ, D), 1.0),
        'cache_na_k': normal((DEC_BATCH, R, PAST_LEN, N_HEADS, HEAD_DIM), 1.0),
        'cache_na_v': normal((DEC_BATCH, R, PAST_LEN, N_HEADS, HEAD_DIM), 1.0),
        'cache_swa_k': normal((DEC_BATCH, R, PAST_LEN, N_KV_HEADS, HEAD_DIM), 1.0),
        'cache_swa_v': normal((DEC_BATCH, R, PAST_LEN, N_KV_HEADS, HEAD_DIM), 1.0),
        'cache_mla_ckv': normal((DEC_BATCH, R, PAST_LEN, MLA_KV_LORA), 1.0),
        'cache_mla_kpe': normal((DEC_BATCH, R, PAST_LEN, MLA_ROPE), 1.0),
        'cache_gqa_k': normal((DEC_BATCH, R, PAST_LEN, N_KV_HEADS, HEAD_DIM), 1.0),
        'cache_gqa_v': normal((DEC_BATCH, R, PAST_LEN, N_KV_HEADS, HEAD_DIM), 1.0),
        'c': normal((DEC_BATCH, D), 1.0),
        'c_ctx': normal((D,), 1.0),
        'norm1_g': gain((DEPTH, D)),
        'norm2_g': gain((DEPTH, D)),
        'w_mod': normal((DEPTH, D, 6 * D), 0.5 * D ** -0.5),
        'b_mod': normal((DEPTH, 6 * D), 0.02),
        'na_w_qkv': normal((R, D, 3 * hq), D ** -0.5),
        'na_q_g': gain((R, HEAD_DIM)),
        'na_k_g': gain((R, HEAD_DIM)),
        'na_rpb': normal((R, N_HEADS, 2 * NA_WIN_ROWS - 1, 2 * NA_WIN_COLS - 1), 0.1),
        'na_w_o': normal((R, hq, D), hq ** -0.5),
        'swa_w_qkv': normal((R, D, hq + 2 * hkv), D ** -0.5),
        'swa_q_g': gain((R, HEAD_DIM)),
        'swa_k_g': gain((R, HEAD_DIM)),
        'swa_sink': normal((R, N_HEADS), 0.5),
        'swa_w_o': normal((R, hq, D), hq ** -0.5),
        'mla_w_dq': normal((R, D, MLA_Q_LORA), D ** -0.5),
        'mla_q_lora_g': gain((R, MLA_Q_LORA)),
        'mla_w_uq': normal((R, MLA_Q_LORA, N_HEADS * (MLA_NOPE + MLA_ROPE)), MLA_Q_LORA ** -0.5),
        'mla_q_nope_g': gain((R, MLA_NOPE)),
        'mla_q_pe_g': gain((R, MLA_ROPE)),
        'mla_w_dkv': normal((R, D, MLA_KV_LORA + MLA_ROPE), D ** -0.5),
        'mla_kv_lora_g': gain((R, MLA_KV_LORA)),
        'mla_k_pe_g': gain((R, MLA_ROPE)),
        'mla_w_ukv': normal((R, MLA_KV_LORA, N_HEADS * (MLA_NOPE + MLA_V)), MLA_KV_LORA ** -0.5),
        'mla_k_nope_g': gain((R, MLA_NOPE)),
        'mla_w_o': normal((R, N_HEADS * MLA_V, D), (N_HEADS * MLA_V) ** -0.5),
        'gqa_w_qkv': normal((R, D, hq + 2 * hkv), D ** -0.5),
        'gqa_q_g': gain((R, HEAD_DIM)),
        'gqa_k_g': gain((R, HEAD_DIM)),
        'gqa_w_o': normal((R, hq, D), hq ** -0.5),
        'ffn_w_gate': normal((DEPTH, D, FFN_HIDDEN), D ** -0.5),
        'ffn_w_up': normal((DEPTH, D, FFN_HIDDEN), D ** -0.5),
        'ffn_w_down': normal((DEPTH, FFN_HIDDEN, D), FFN_HIDDEN ** -0.5),
    }


def reference(x_prompt, x_sample, cache_na_k, cache_na_v, cache_swa_k, cache_swa_v, cache_mla_ckv,
              cache_mla_kpe, cache_gqa_k, cache_gqa_v, c, c_ctx, norm1_g, norm2_g, w_mod, b_mod,
              na_w_qkv, na_q_g, na_k_g, na_rpb, na_w_o, swa_w_qkv, swa_q_g, swa_k_g, swa_sink, swa_w_o,
              mla_w_dq, mla_q_lora_g, mla_w_uq, mla_q_nope_g, mla_q_pe_g, mla_w_dkv, mla_kv_lora_g,
              mla_k_pe_g, mla_w_ukv, mla_k_nope_g, mla_w_o, gqa_w_qkv, gqa_q_g, gqa_k_g, gqa_w_o,
              ffn_w_gate, ffn_w_up, ffn_w_down):
    rows, cols = grid_positions(x_sample.shape[1])
    xp, xs = x_prompt, x_sample
    new_na_k, new_na_v, new_swa_k, new_swa_v = [], [], [], []
    new_mla_ckv, new_mla_kpe, new_gqa_k, new_gqa_v = [], [], [], []
    for li in range(DEPTH):
        kind, r = li % N_MIXERS, li // N_MIXERS
        mp = adaln(c_ctx[None, :], w_mod[li], b_mod[li])
        ms = adaln(c, w_mod[li], b_mod[li])
        hp = modulate(rms_norm(xp, norm1_g[li]), mp[0], mp[1])
        hs = modulate(rms_norm(xs, norm1_g[li]), ms[0], ms[1])
        if kind == 0:
            qp, kp_, vp_ = mha_project(hp, na_w_qkv[r], na_q_g[r], na_k_g[r], N_HEADS)
            op = heads_out(block_attention(qp, kp_, vp_), na_w_o[r])
            qs, ks_, vs_ = mha_project(hs, na_w_qkv[r], na_q_g[r], na_k_g[r], N_HEADS)
            os_ = heads_out(na_latent_attention(qs, ks_, vs_, cache_na_k[:, r], cache_na_v[:, r], na_rpb[r]), na_w_o[r])
            new_na_k.append(kp_)
            new_na_v.append(vp_)
        elif kind == 1:
            qp, kp_, vp_ = mha_project(hp, swa_w_qkv[r], swa_q_g[r], swa_k_g[r], N_KV_HEADS)
            op = heads_out(block_attention(qp, kp_, vp_, sink=swa_sink[r]), swa_w_o[r])
            qs, ks_, vs_ = mha_project(hs, swa_w_qkv[r], swa_q_g[r], swa_k_g[r], N_KV_HEADS)
            qs, ks_ = axial_rope(qs, rows, cols), axial_rope(ks_, rows, cols)
            os_ = heads_out(swa_latent_attention(qs, ks_, vs_, cache_swa_k[:, r], cache_swa_v[:, r], swa_sink[r]), swa_w_o[r])
            new_swa_k.append(kp_)
            new_swa_v.append(vp_)
        elif kind == 2:
            qn, qpe = mla_queries(hp, mla_w_dq[r], mla_q_lora_g[r], mla_w_uq[r], mla_q_nope_g[r], mla_q_pe_g[r])
            ckv, kpe = mla_compress(hp, mla_w_dkv[r], mla_kv_lora_g[r], mla_k_pe_g[r])
            k_ctx, v_ctx = mla_expand(ckv, kpe[:, :, None, :], mla_w_ukv[r], mla_k_nope_g[r])
            op = heads_out(block_attention(jnp.concatenate([qn, qpe], axis=-1), k_ctx, v_ctx), mla_w_o[r])
            qn_s, qpe_s = mla_queries(hs, mla_w_dq[r], mla_q_lora_g[r], mla_w_uq[r], mla_q_nope_g[r], mla_q_pe_g[r])
            q_s = jnp.concatenate([qn_s, axial_rope(qpe_s, rows, cols)], axis=-1)
            ckv_s, kpe_s = mla_compress(hs, mla_w_dkv[r], mla_kv_lora_g[r], mla_k_pe_g[r])
            k_lat, v_lat = mla_expand(ckv_s, axial_rope(kpe_s[:, :, None, :], rows, cols), mla_w_ukv[r], mla_k_nope_g[r])
            k_cc, v_cc = mla_expand(cache_mla_ckv[:, r], cache_mla_kpe[:, r][:, :, None, :], mla_w_ukv[r], mla_k_nope_g[r])
            os_ = heads_out(block_attention(q_s, jnp.concatenate([k_lat, k_cc], axis=1),
                                            jnp.concatenate([v_lat, v_cc], axis=1)), mla_w_o[r])
            new_mla_ckv.append(ckv)
            new_mla_kpe.append(kpe)
        else:
            qp, kp_, vp_ = mha_project(hp, gqa_w_qkv[r], gqa_q_g[r], gqa_k_g[r], N_KV_HEADS)
            op = heads_out(block_attention(qp, kp_, vp_), gqa_w_o[r])
            qs, ks_, vs_ = mha_project(hs, gqa_w_qkv[r], gqa_q_g[r], gqa_k_g[r], N_KV_HEADS)
            qs, ks_ = axial_rope(qs, rows, cols), axial_rope(ks_, rows, cols)
            os_ = heads_out(block_attention(qs, jnp.concatenate([ks_, cache_gqa_k[:, r]], axis=1),
                                            jnp.concatenate([vs_, cache_gqa_v[:, r]], axis=1)), gqa_w_o[r])
            new_gqa_k.append(kp_)
            new_gqa_v.append(vp_)
        xp = xp + mp[2] * op
        xs = xs + ms[2] * os_
        xp = xp + mp[5] * swiglu(modulate(rms_norm(xp, norm2_g[li]), mp[3], mp[4]), ffn_w_gate[li], ffn_w_up[li], ffn_w_down[li])
        xs = xs + ms[5] * swiglu(modulate(rms_norm(xs, norm2_g[li]), ms[3], ms[4]), ffn_w_gate[li], ffn_w_up[li], ffn_w_down[li])
    return (xp, xs, jnp.stack(new_na_k, axis=1), jnp.stack(new_na_v, axis=1), jnp.stack(new_swa_k, axis=1),
            jnp.stack(new_swa_v, axis=1), jnp.stack(new_mla_ckv, axis=1), jnp.stack(new_mla_kpe, axis=1),
            jnp.stack(new_gqa_k, axis=1), jnp.stack(new_gqa_v, axis=1))
```

```python
import functools

import numpy as np
import jax
import jax.numpy as jnp
from jax import lax
from jax.experimental import pallas as pl
from jax.experimental.pallas import tpu as pltpu

GRID_W = 64
HEAD_DIM = 64
N_HEADS = 16
N_KV_HEADS = 4
NA_WIN_ROWS = 8
NA_WIN_COLS = 16
SWA_WINDOW = 128
MLA_KV_LORA = 128
MLA_NOPE = 64
MLA_ROPE = 32
MLA_V = 64
ROPE_THETA = 10000.0
EPS = 1e-6
NEG_INF = -1e30

LANES = 128
MXU_COLS = 256
VMEM_LIMIT = 56 * 1024 * 1024
TOKEN_TILE = 256
Q_TILE = 256

F32 = jnp.float32
BF16 = jnp.bfloat16


def _params(*sem):
    return pltpu.CompilerParams(dimension_semantics=sem, vmem_limit_bytes=VMEM_LIMIT)


def _resident(shape):
    nd = len(shape)
    return pl.BlockSpec(shape, lambda *_: (0,) * nd, pipeline_mode=pl.Buffered(1))


def _dot(a, b):
    return jnp.dot(a, b, preferred_element_type=F32)


def _dot_nt(a, b):
    return lax.dot_general(a, b, (((1,), (1,)), ((), ())), preferred_element_type=F32)


def _norm_mod(x, g, shift, scale):
    ms = jnp.mean(x * x, axis=-1, keepdims=True)
    y = x * lax.rsqrt(ms + EPS) * g
    return y * (1.0 + scale) + shift


def _row_rms(x, g, n):
    ms = jnp.sum(x * x, axis=-1, keepdims=True) * (1.0 / n)
    return x * lax.rsqrt(ms + EPS) * g


def _group_rms(y, gain, bd, group):
    sq = y * y
    hi = sq.astype(BF16)
    lo = (sq - hi.astype(F32)).astype(BF16)
    ss = _dot(hi, bd) + _dot(lo, bd)
    return y * lax.rsqrt(ss * (1.0 / group) + EPS) * gain


def _rope(y, cos, sin_signed, quarter):
    n = y.shape[-1]
    lane = lax.broadcasted_iota(jnp.int32, (1, n), 1)
    first = (lane & quarter) == 0
    partner = jnp.where(first, pltpu.roll(y, n - quarter, 1), pltpu.roll(y, quarter, 1))
    return y * cos + partner * sin_signed


def _adaln_kernel(cond_ref, w_ref, b_ref, o_ref):
    c = cond_ref[...]
    a = (c * jax.nn.sigmoid(c)).astype(BF16)
    o_ref[0] = _dot(a, w_ref[0].astype(BF16)) + b_ref[0]


def _adaln(cond, w_mod, b_mod):
    depth, d, n = w_mod.shape
    rows = cond.shape[0]
    tn = 1536
    return pl.pallas_call(
        _adaln_kernel,
        out_shape=jax.ShapeDtypeStruct((depth, rows, n), F32),
        grid=(depth, n // tn),
        in_specs=[
            pl.BlockSpec((rows, d), lambda l, j: (0, 0)),
            pl.BlockSpec((1, d, tn), lambda l, j: (l, 0, j)),
            pl.BlockSpec((1, 1, tn), lambda l, j: (l, 0, j)),
        ],
        out_specs=pl.BlockSpec((1, rows, tn), lambda l, j: (l, 0, j)),
        compiler_params=_params("parallel", "parallel"),
        name="adaln",
    )(cond, w_mod, b_mod.reshape(depth, 1, n))


def _mod_spec(mod):
    if mod.shape[0] == 1:
        return pl.BlockSpec((1,) + mod.shape[1:], lambda b, i: (0, 0, 0))
    return pl.BlockSpec((1,) + mod.shape[1:], lambda b, i: (b, 0, 0))


def _qkv_kernel(*refs, nq, nk, rope):
    x_ref, g_ref, mod_ref, w_ref, qg_ref, kg_ref, bd_ref = refs[:7]
    if rope:
        cos, sin = refs[7][...], refs[8][...]
        q_ref, k_ref, v_ref = refs[9:]
    else:
        cos = sin = None
        q_ref, k_ref, v_ref = refs[7:]
    h = _norm_mod(x_ref[0], g_ref[...], mod_ref[0, 0:1, :], mod_ref[0, 1:2, :]).astype(BF16)
    bd = bd_ref[...]

    def head(col, gain):
        y = _group_rms(_dot(h, w_ref[:, col:col + MXU_COLS]), gain, bd, HEAD_DIM)
        return _rope(y, cos, sin, HEAD_DIM // 4) if rope else y

    for c in range(0, nq, MXU_COLS):
        q_ref[0, :, c:c + MXU_COLS] = head(c, qg_ref[...]).astype(q_ref.dtype)
    for c in range(0, nk, MXU_COLS):
        k_ref[0, :, c:c + MXU_COLS] = head(nq + c, kg_ref[...]).astype(k_ref.dtype)
    for c in range(0, nk, MXU_COLS):
        col = nq + nk + c
        v_ref[0, :, c:c + MXU_COLS] = _dot(h, w_ref[:, col:col + MXU_COLS]).astype(v_ref.dtype)


def _qkv_project(x, g, mod, w, q_gain, k_gain, nq, nk, kv_dtype, rope_tables=None):
    b, l, d = x.shape
    tl = TOKEN_TILE
    rope = rope_tables is not None
    bd = _block_diag_ones(MXU_COLS, HEAD_DIM)
    args = [x, g.reshape(1, d), mod, w, _tile_gain(q_gain), _tile_gain(k_gain), bd]
    in_specs = [
        pl.BlockSpec((1, tl, d), lambda b_, i: (b_, i, 0)),
        _resident((1, d)),
        _mod_spec(mod),
        _resident(w.shape),
        _resident((1, MXU_COLS)),
        _resident((1, MXU_COLS)),
        _resident(bd.shape),
    ]
    if rope:
        args += list(rope_tables)
        in_specs += [pl.BlockSpec((tl, MXU_COLS), lambda b_, i: (i, 0))] * 2
    out = lambda n, dt: (jax.ShapeDtypeStruct((b, l, n), dt),
                         pl.BlockSpec((1, tl, n), lambda b_, i: (b_, i, 0)))
    shapes, specs = zip(out(nq, BF16), out(nk, kv_dtype), out(nk, kv_dtype))
    return pl.pallas_call(
        functools.partial(_qkv_kernel, nq=nq, nk=nk, rope=rope),
        out_shape=shapes,
        grid=(b, l // tl),
        in_specs=in_specs,
        out_specs=specs,
        compiler_params=_params("parallel", "parallel"),
        name="qkv_project",
    )(*args)


def _mla_project_kernel(*refs, rope):
    (x_ref, g_ref, mod_ref, wdq_ref, qlg_ref, wuq_ref, qng_ref, qpg_ref, bdn_ref, bdp_ref,
     wdkv_ref, kvg_ref, kpg_ref) = refs[:13]
    if rope:
        cos, sin = refs[13][...], refs[14][...]
        qn_ref, qp_ref, ckv_ref, kpe_ref = refs[15:]
    else:
        cos = sin = None
        qn_ref, qp_ref, ckv_ref, kpe_ref = refs[13:]
    h = _norm_mod(x_ref[0], g_ref[...], mod_ref[0, 0:1, :], mod_ref[0, 1:2, :]).astype(BF16)
    q_lora = wdq_ref.shape[1]
    ql = _row_rms(_dot(h, wdq_ref[...]), qlg_ref[...], q_lora).astype(BF16)
    n_nope = qn_ref.shape[2]
    n_pe = qp_ref.shape[2]
    for c in range(0, n_nope, MXU_COLS):
        y = _dot(ql, wuq_ref[:, c:c + MXU_COLS])
        qn_ref[0, :, c:c + MXU_COLS] = _group_rms(y, qng_ref[...], bdn_ref[...], MLA_NOPE).astype(qn_ref.dtype)
    for c in range(0, n_pe, MXU_COLS):
        y = _dot(ql, wuq_ref[:, n_nope + c:n_nope + c + MXU_COLS])
        y = _group_rms(y, qpg_ref[...], bdp_ref[...], MLA_ROPE)
        if rope:
            y = _rope(y, cos, sin, MLA_ROPE // 4)
        qp_ref[0, :, c:c + MXU_COLS] = y.astype(qp_ref.dtype)
    ckv = _dot(h, wdkv_ref[:, :MLA_KV_LORA])
    ckv_ref[0] = _row_rms(ckv, kvg_ref[...], MLA_KV_LORA)
    kpe = _row_rms(_dot(h, wdkv_ref[:, MLA_KV_LORA:]), kpg_ref[...], MLA_ROPE)
    if rope:
        kpe = _rope(kpe, cos[:, :LANES], sin[:, :LANES], MLA_ROPE // 4)
    kpe_ref[0] = kpe


def _mla_project(x, g, mod, w_dq, q_lora_g, w_uq, q_nope_g, q_pe_g, w_dkv, kv_lora_g, k_pe_g,
                 rope_tables=None):
    b, l, d = x.shape
    tl = TOKEN_TILE
    rope = rope_tables is not None
    n_nope, n_pe = N_HEADS * MLA_NOPE, N_HEADS * MLA_ROPE
    bdn = _block_diag_ones(MXU_COLS, MLA_NOPE)
    bdp = _block_diag_ones(MXU_COLS, MLA_ROPE)
    kpg = jnp.pad(k_pe_g, (0, LANES - MLA_ROPE)).reshape(1, LANES)
    args = [x, g.reshape(1, d), mod, w_dq, q_lora_g.reshape(1, -1), w_uq, _tile_gain(q_nope_g),
            _tile_gain(q_pe_g), bdn, bdp, w_dkv, kv_lora_g.reshape(1, -1), kpg]
    in_specs = [pl.BlockSpec((1, tl, d), lambda b_, i: (b_, i, 0)), _resident((1, d)), _mod_spec(mod)]
    in_specs += [_resident(a.shape) for a in args[3:]]
    if rope:
        args += list(rope_tables)
        in_specs += [pl.BlockSpec((tl, MXU_COLS), lambda b_, i: (i, 0))] * 2
    out = lambda n, dt: (jax.ShapeDtypeStruct((b, l, n), dt),
                         pl.BlockSpec((1, tl, n), lambda b_, i: (b_, i, 0)))
    shapes, specs = zip(out(n_nope, BF16), out(n_pe, BF16), out(MLA_KV_LORA, F32), out(LANES, F32))
    return pl.pallas_call(
        functools.partial(_mla_project_kernel, rope=rope),
        out_shape=shapes,
        grid=(b, l // tl),
        in_specs=in_specs,
        out_specs=specs,
        compiler_params=_params("parallel", "parallel"),
        name="mla_project",
    )(*args)


def _mla_expand_kernel(ckv_ref, kpe_ref, w_ref, kng_ref, bd_ref, tile_ref, kn_ref, v_ref, kpt_ref):
    c = ckv_ref[0].astype(BF16)
    n = kn_ref.shape[2]
    for col in range(0, n, MXU_COLS):
        y = _dot(c, w_ref[:, col:col + MXU_COLS])
        kn_ref[0, :, col:col + MXU_COLS] = _group_rms(y, kng_ref[...], bd_ref[...], MLA_NOPE).astype(kn_ref.dtype)
    for col in range(0, n, MXU_COLS):
        v_ref[0, :, col:col + MXU_COLS] = _dot(c, w_ref[:, n + col:n + col + MXU_COLS]).astype(v_ref.dtype)
    kpt_ref[0] = _dot(kpe_ref[0].astype(BF16), tile_ref[...]).astype(kpt_ref.dtype)


def _mla_expand(ckv, kpe, w_ukv, k_nope_g):
    b, l, _ = ckv.shape
    tl = TOKEN_TILE
    n = N_HEADS * MLA_NOPE
    pw = kpe.shape[2]
    bd = _block_diag_ones(MXU_COLS, MLA_NOPE)
    tile = np.zeros((pw, LANES), np.float32)
    for r in range(LANES // MLA_ROPE):
        tile[np.arange(MLA_ROPE), r * MLA_ROPE + np.arange(MLA_ROPE)] = 1.0
    tile = jnp.asarray(tile, BF16)
    out = lambda m: (jax.ShapeDtypeStruct((b, l, m), BF16), pl.BlockSpec((1, tl, m), lambda b_, i: (b_, i, 0)))
    shapes, specs = zip(out(n), out(n), out(LANES))
    return pl.pallas_call(
        _mla_expand_kernel,
        out_shape=shapes,
        grid=(b, l // tl),
        in_specs=[
            pl.BlockSpec((1, tl, MLA_KV_LORA), lambda b_, i: (b_, i, 0)),
            pl.BlockSpec((1, tl, pw), lambda b_, i: (b_, i, 0)),
            _resident(w_ukv.shape),
            _resident((1, MXU_COLS)),
            _resident(bd.shape),
            _resident(tile.shape),
        ],
        out_specs=specs,
        compiler_params=_params("parallel", "parallel"),
        name="mla_expand",
    )(ckv, kpe, w_ukv, _tile_gain(k_nope_g), bd, tile)


def _attn_kernel(*refs, mode, ctx, mla, sink, tq, rows):
    it = iter(refs)
    q_ref, k_ref, v_ref = next(it), next(it), next(it)
    kc_ref, vc_ref = (next(it), next(it)) if ctx else (None, None)
    q2_ref, k2_ref = (next(it), next(it)) if mla else (None, None)
    kc2_ref = next(it) if (mla and ctx) else None
    sink_ref = next(it) if sink else None
    tbl_ref = next(it) if mode == "na" else None
    o_ref = next(it)

    p = pl.program_id(1)
    t = pl.program_id(2)
    lane = lax.broadcasted_iota(jnp.int32, (1, LANES), 1)

    def stack(x, m0, m1):
        zero = jnp.zeros_like(x)
        return jnp.concatenate([jnp.where(m0, x, zero), jnp.where(m1, x, zero)], axis=0)

    qs = stack(q_ref[0].astype(BF16), lane < HEAD_DIM, lane >= HEAD_DIM)
    if mla:
        base = (p % 2) * (2 * MLA_ROPE)
        m0 = (lane >= base) & (lane < base + MLA_ROPE)
        m1 = (lane >= base + MLA_ROPE) & (lane < base + 2 * MLA_ROPE)
        qs2 = stack(q2_ref[0].astype(BF16), m0, m1)

    if mode == "full":
        kk, vv = k_ref[0].astype(BF16), v_ref[0].astype(BF16)
    elif mode == "swa":
        win = tq + 2 * SWA_WINDOW
        ws = pl.multiple_of(jnp.clip(t * tq - SWA_WINDOW, 0, k_ref.shape[1] - win), SWA_WINDOW)
        kk, vv = k_ref[0, pl.ds(ws, win), :], v_ref[0, pl.ds(ws, win), :]
    else:
        qrows = tq // GRID_W
        wrows = qrows + NA_WIN_ROWS
        r0 = t * qrows
        ws_row = jnp.clip(r0 - NA_WIN_ROWS // 2, 0, rows - wrows)
        ws = pl.multiple_of(ws_row * GRID_W, GRID_W)
        kk, vv = k_ref[0, pl.ds(ws, wrows * GRID_W), :], v_ref[0, pl.ds(ws, wrows * GRID_W), :]

    s = _dot_nt(qs, kk)
    if mla:
        s = s + _dot_nt(qs2, k2_ref[0].astype(BF16))

    if mode == "swa":
        kpos = ws + lax.broadcasted_iota(jnp.int32, (1, win), 1)
        qi = t * tq + lax.broadcasted_iota(jnp.int32, (tq, 1), 0)
        qpos = jnp.concatenate([qi, qi], axis=0)
        s = jnp.where(jnp.abs(kpos - qpos) <= SWA_WINDOW, s, NEG_INF)
    elif mode == "na":
        krow = ws_row + lax.broadcasted_iota(jnp.int32, (1, wrows * GRID_W), 1) // GRID_W
        e0 = ws_row - r0 + (NA_WIN_ROWS - 1) + qrows
        blocks = []
        for hh in range(2):
            for i in range(qrows):
                rs = jnp.clip(r0 + i - NA_WIN_ROWS // 2, 0, rows - NA_WIN_ROWS)
                valid = (krow >= rs) & (krow < rs + NA_WIN_ROWS)
                bias = jnp.concatenate(
                    [tbl_ref[hh, e0 + 2 * j - i] for j in range(wrows // 2)], axis=1)
                lo = hh * tq + i * GRID_W
                blocks.append(jnp.where(valid, s[lo:lo + GRID_W, :] + bias, NEG_INF))
        s = jnp.concatenate(blocks, axis=0)

    m = jnp.max(s, axis=-1, keepdims=True)
    if ctx:
        sc = _dot_nt(qs, kc_ref[0].astype(BF16))
        if mla:
            sc = sc + _dot_nt(qs2, kc2_ref[0].astype(BF16))
        m = jnp.maximum(m, jnp.max(sc, axis=-1, keepdims=True))
    if sink:
        first = lax.broadcasted_iota(jnp.int32, (2 * tq, 1), 0) < tq
        sk = jnp.where(first, sink_ref[2 * p], sink_ref[2 * p + 1])
        m = jnp.maximum(m, sk)
    pr = jnp.exp(s - m)
    den = jnp.sum(pr, axis=-1, keepdims=True)
    o = _dot(pr.astype(BF16), vv)
    if ctx:
        pc = jnp.exp(sc - m)
        den = den + jnp.sum(pc, axis=-1, keepdims=True)
        o = o + _dot(pc.astype(BF16), vc_ref[0].astype(BF16))
    if sink:
        den = den + jnp.exp(sk - m)
    o = o / den
    o_ref[0] = jnp.where(lane < HEAD_DIM, o[:tq], o[tq:]).astype(o_ref.dtype)


def _attention(q, k, v, *, mode="full", kc=None, vc=None, q2=None, k2=None, kc2=None, sink=None,
               na_table=None):
    b, lq, nq = q.shape
    lk = k.shape[1]
    pairs = nq // LANES
    group = pairs // (k.shape[2] // LANES)
    tq = min(Q_TILE, lq)
    ctx, mla = kc is not None, q2 is not None

    qmap = lambda b_, p, t: (b_, t, p)
    kvmap = lambda b_, p, t: (b_, 0, p // group)
    args = [q, k, v]
    in_specs = [pl.BlockSpec((1, tq, LANES), qmap),
                pl.BlockSpec((1, lk, LANES), kvmap),
                pl.BlockSpec((1, lk, LANES), kvmap)]
    if ctx:
        lc = kc.shape[1]
        args += [kc, vc]
        in_specs += [pl.BlockSpec((1, lc, LANES), kvmap)] * 2
    if mla:
        args += [q2, k2]
        in_specs += [pl.BlockSpec((1, tq, LANES), lambda b_, p, t: (b_, t, p // 2)),
                     pl.BlockSpec((1, lk, LANES), lambda b_, p, t: (b_, 0, 0))]
        if ctx:
            args += [kc2]
            in_specs += [pl.BlockSpec((1, kc2.shape[1], LANES), lambda b_, p, t: (b_, 0, 0))]
    if sink is not None:
        args += [sink]
        in_specs += [pl.BlockSpec(memory_space=pltpu.SMEM)]
    if mode == "na":
        args += [na_table]
        in_specs += [pl.BlockSpec((2,) + na_table.shape[1:], lambda b_, p, t: (p, 0, 0, 0))]
    return pl.pallas_call(
        functools.partial(_attn_kernel, mode=mode, ctx=ctx, mla=mla, sink=sink is not None, tq=tq,
                          rows=lq // GRID_W),
        out_shape=jax.ShapeDtypeStruct((b, lq, nq), BF16),
        grid=(b, pairs, lq // tq),
        in_specs=in_specs,
        out_specs=pl.BlockSpec((1, tq, LANES), qmap),
        compiler_params=_params("parallel", "parallel", "parallel"),
        name="attention_" + mode,
    )(*args)


def _out_ffn_kernel(x_ref, a_ref, mod_ref, g_ref, wo_ref, wg_ref, wu_ref, wd_ref, o_ref, *, chunks):
    mod = lambda r: mod_ref[0, r:r + 1, :]
    x1 = x_ref[0] + mod(2) * _dot(a_ref[0], wo_ref[...])
    h = _norm_mod(x1, g_ref[...], mod(3), mod(4)).astype(BF16)
    hidden = wg_ref.shape[1]
    step = hidden // chunks
    y = None
    for c in range(0, hidden, step):
        gate = _dot(h, wg_ref[:, c:c + step])
        up = _dot(h, wu_ref[:, c:c + step])
        act = (gate * jax.nn.sigmoid(gate) * up).astype(BF16)
        part = _dot(act, wd_ref[c:c + step, :])
        y = part if y is None else y + part
    o_ref[0] = x1 + mod(5) * y


def _out_ffn(x, attn, mod, g2, w_o, w_gate, w_up, w_down):
    b, l, d = x.shape
    tl = TOKEN_TILE
    tok = lambda n: pl.BlockSpec((1, tl, n), lambda b_, i: (b_, i, 0))
    return pl.pallas_call(
        functools.partial(_out_ffn_kernel, chunks=2),
        out_shape=jax.ShapeDtypeStruct((b, l, d), F32),
        grid=(b, l // tl),
        in_specs=[tok(d), tok(attn.shape[2]), _mod_spec(mod), _resident((1, d)),
                  _resident(w_o.shape), _resident(w_gate.shape), _resident(w_up.shape),
                  _resident(w_down.shape)],
        out_specs=tok(d),
        compiler_params=_params("parallel", "parallel"),
        name="out_ffn",
    )(x, attn, mod, g2.reshape(1, d), w_o, w_gate, w_up, w_down)


def _block_diag_ones(n, group):
    idx = np.arange(n) // group
    return jnp.asarray(idx[:, None] == idx[None, :], BF16)


def _tile_gain(g, scale=1.0):
    return jnp.tile(g * scale, MXU_COLS // g.shape[0]).reshape(1, MXU_COLS)


def _rope_tables(n_tokens, r):
    half, quarter = r // 2, r // 4
    t = jnp.arange(n_tokens, dtype=jnp.int32)
    rows, cols = (t // GRID_W).astype(F32), (t % GRID_W).astype(F32)
    inv_freq = ROPE_THETA ** (-jnp.arange(quarter, dtype=F32) / quarter)
    d = np.arange(MXU_COLS) % r
    use_row = jnp.asarray(d < half)
    freq = inv_freq[d % quarter]
    ang = jnp.where(use_row[None, :], rows[:, None], cols[:, None]) * freq[None, :]
    sign = jnp.asarray(np.where((d % half) < quarter, -1.0, 1.0), F32)
    return jnp.cos(ang), jnp.sin(ang) * sign[None, :]


_GQA_HEADS = np.array([8 * (p // 4) + 4 * e + (p % 4) for p in range(N_HEADS // 2) for e in range(2)])
_GQA_COLS = (_GQA_HEADS[:, None] * HEAD_DIM + np.arange(HEAD_DIM)[None, :]).reshape(-1)


def _na_bias_table(rpb, q_rows):
    qc = np.arange(GRID_W)
    start = np.clip(qc - NA_WIN_COLS // 2, 0, GRID_W - NA_WIN_COLS)
    kc = np.arange(GRID_W)
    valid = (kc[None, :] >= start[:, None]) & (kc[None, :] < start[:, None] + NA_WIN_COLS)
    dcol = np.clip(kc[None, :] - qc[:, None] + NA_WIN_COLS - 1, 0, 2 * NA_WIN_COLS - 2)
    t = jnp.where(jnp.asarray(valid)[None, None], rpb[:, :, dcol], NEG_INF)
    n_dr = 2 * NA_WIN_ROWS - 1
    pad = q_rows + 1
    t = jnp.pad(t, ((0, 0), (pad, pad), (0, 0), (0, 0)))
    n_e = 2 * q_rows + 2 * NA_WIN_ROWS - 2
    lo = t[:, 1:1 + n_e]
    hi = t[:, 2:2 + n_e]
    del n_dr
    return jnp.concatenate([lo, hi], axis=-1)


def kernel(x_prompt, x_sample, cache_na_k, cache_na_v, cache_swa_k, cache_swa_v, cache_mla_ckv, cache_mla_kpe, cache_gqa_k, cache_gqa_v, c, c_ctx, norm1_g, norm2_g, w_mod, b_mod, na_w_qkv, na_q_g, na_k_g, na_rpb, na_w_o, swa_w_qkv, swa_q_g, swa_k_g, swa_sink, swa_w_o, mla_w_dq, mla_q_lora_g, mla_w_uq, mla_q_nope_g, mla_q_pe_g, mla_w_dkv, mla_kv_lora_g, mla_k_pe_g, mla_w_ukv, mla_k_nope_g, mla_w_o, gqa_w_qkv, gqa_q_g, gqa_k_g, gqa_w_o, ffn_w_gate, ffn_w_up, ffn_w_down):
    depth, d = norm1_g.shape
    bp, lp, _ = x_prompt.shape
    bs, ls, _ = x_sample.shape
    hq, hkv = N_HEADS * HEAD_DIM, N_KV_HEADS * HEAD_DIM
    scale = HEAD_DIM ** -0.5

    cond = jnp.concatenate([c_ctx[None, :], c, jnp.zeros((16 - 1 - bs, d), F32)], axis=0)
    mods = _adaln(cond, w_mod, b_mod).reshape(depth, 16, 6, d)
    rope64 = _rope_tables(ls, HEAD_DIM)
    rope32 = _rope_tables(ls, MLA_ROPE)
    flat = lambda a, r: a[:, r].reshape(a.shape[0], a.shape[2], -1)

    xp, xs = x_prompt, x_sample
    outs = {}
    for li in range(depth):
        kind, r = li % 4, li // 4
        mp, ms = mods[li, 0:1], mods[li, 1:1 + bs]
        g1 = norm1_g[li]
        if kind in (0, 1, 3):
            w_qkv, q_g, k_g, w_o = ((na_w_qkv, na_q_g, na_k_g, na_w_o), (swa_w_qkv, swa_q_g, swa_k_g, swa_w_o),
                                    None, (gqa_w_qkv, gqa_q_g, gqa_k_g, gqa_w_o))[kind]
            w_qkv, w_o = w_qkv[r], w_o[r]
            nk = hq if kind == 0 else hkv
            if kind != 0:
                w_qkv = jnp.concatenate([w_qkv[:, :hq][:, _GQA_COLS], w_qkv[:, hq:]], axis=1)
                w_o = w_o[_GQA_COLS, :]
            w_qkv, w_o = w_qkv.astype(BF16), w_o.astype(BF16)
            qp, kp, vp = _qkv_project(xp, g1, mp, w_qkv, q_g[r] * scale, k_g[r], hq, nk, F32)
            qs, ks, vs = _qkv_project(xs, g1, ms, w_qkv, q_g[r] * scale, k_g[r], hq, nk, BF16,
                                      rope_tables=None if kind == 0 else rope64)
            if kind == 0:
                kc, vc = flat(cache_na_k, r), flat(cache_na_v, r)
                ap = _attention(qp, kp, vp)
                as_ = _attention(qs, ks, vs, mode="na", kc=kc, vc=vc,
                                 na_table=_na_bias_table(na_rpb[r], Q_TILE // GRID_W))
                outs["na_k"], outs["na_v"] = kp, vp
            elif kind == 1:
                kc, vc = flat(cache_swa_k, r), flat(cache_swa_v, r)
                sink = swa_sink[r][_GQA_HEADS]
                ap = _attention(qp, kp, vp, sink=sink)
                as_ = _attention(qs, ks, vs, mode="swa", kc=kc, vc=vc, sink=sink)
                outs["swa_k"], outs["swa_v"] = kp, vp
            else:
                kc, vc = flat(cache_gqa_k, r), flat(cache_gqa_v, r)
                ap = _attention(qp, kp, vp)
                as_ = _attention(qs, ks, vs, kc=kc, vc=vc)
                outs["gqa_k"], outs["gqa_v"] = kp, vp
        else:
            dk = MLA_NOPE + MLA_ROPE
            w_uq = mla_w_uq[r].reshape(-1, N_HEADS, dk)
            w_uq = jnp.concatenate([w_uq[:, :, :MLA_NOPE].reshape(-1, N_HEADS * MLA_NOPE),
                                    w_uq[:, :, MLA_NOPE:].reshape(-1, N_HEADS * MLA_ROPE)], axis=1).astype(BF16)
            w_ukv = mla_w_ukv[r].reshape(-1, N_HEADS, MLA_NOPE + MLA_V)
            w_ukv = jnp.concatenate([w_ukv[:, :, :MLA_NOPE].reshape(-1, N_HEADS * MLA_NOPE),
                                     w_ukv[:, :, MLA_NOPE:].reshape(-1, N_HEADS * MLA_V)], axis=1).astype(BF16)
            w_dkv = jnp.pad(mla_w_dkv[r], ((0, 0), (0, 2 * LANES - MLA_KV_LORA - MLA_ROPE))).astype(BF16)
            w_dq, w_o = mla_w_dq[r].astype(BF16), mla_w_o[r].astype(BF16)
            qscale = dk ** -0.5
            proj = functools.partial(
                _mla_project, w_dq=w_dq, q_lora_g=mla_q_lora_g[r], w_uq=w_uq, q_nope_g=mla_q_nope_g[r] * qscale,
                q_pe_g=mla_q_pe_g[r] * qscale, w_dkv=w_dkv, kv_lora_g=mla_kv_lora_g[r], k_pe_g=mla_k_pe_g[r])
            qn_p, qpe_p, ckv_p, kpe_p = proj(xp, g1, mp)
            qn_s, qpe_s, ckv_s, kpe_s = proj(xs, g1, ms, rope_tables=rope32)
            kn_p, v_p, kpt_p = _mla_expand(ckv_p, kpe_p, w_ukv, mla_k_nope_g[r])
            kn_s, v_s, kpt_s = _mla_expand(ckv_s, kpe_s, w_ukv, mla_k_nope_g[r])
            kn_c, v_c, kpt_c = _mla_expand(cache_mla_ckv[:, r], cache_mla_kpe[:, r], w_ukv, mla_k_nope_g[r])
            ap = _attention(qn_p, kn_p, v_p, q2=qpe_p, k2=kpt_p)
            as_ = _attention(qn_s, kn_s, v_s, kc=kn_c, vc=v_c, q2=qpe_s, k2=kpt_s, kc2=kpt_c)
            outs["mla_ckv"], outs["mla_kpe"] = ckv_p, kpe_p[:, :, :MLA_ROPE]
        wg, wu, wd = ffn_w_gate[li].astype(BF16), ffn_w_up[li].astype(BF16), ffn_w_down[li].astype(BF16)
        xp = _out_ffn(xp, ap, mp, norm2_g[li], w_o, wg, wu, wd)
        xs = _out_ffn(xs, as_, ms, norm2_g[li], w_o, wg, wu, wd)

    heads = lambda a, h: a.reshape(bp, 1, lp, h, HEAD_DIM)
    return (xp, xs, heads(outs["na_k"], N_HEADS), heads(outs["na_v"], N_HEADS),
            heads(outs["swa_k"], N_KV_HEADS), heads(outs["swa_v"], N_KV_HEADS),
            outs["mla_ckv"][:, None], outs["mla_kpe"][:, None],
            heads(outs["gqa_k"], N_KV_HEADS), heads(outs["gqa_v"], N_KV_HEADS))
```

```python
import functools

import numpy as np
import jax
import jax.numpy as jnp
from jax import lax
from jax.experimental import pallas as pl
from jax.experimental.pallas import tpu as pltpu

GRID_W = 64
HEAD_DIM = 64
N_HEADS = 16
N_KV_HEADS = 4
NA_WIN_ROWS = 8
NA_WIN_COLS = 16
SWA_WINDOW = 128
MLA_KV_LORA = 128
MLA_NOPE = 64
MLA_ROPE = 32
MLA_V = 64
ROPE_THETA = 10000.0
EPS = 1e-6
NEG_INF = -1e30

LANES = 128
MXU_COLS = 256
VMEM_LIMIT = 56 * 1024 * 1024
TOKEN_TILE = 512
Q_TILE = 512
KEY_CHUNK = 512
LOG2E = 1.4426950408889634

F32 = jnp.float32
BF16 = jnp.bfloat16


def _params(*sem):
    return pltpu.CompilerParams(dimension_semantics=sem, vmem_limit_bytes=VMEM_LIMIT)


def _resident(shape):
    nd = len(shape)
    return pl.BlockSpec(shape, lambda *_: (0,) * nd, pipeline_mode=pl.Buffered(1))


def _dot(a, b):
    return jnp.dot(a, b, preferred_element_type=F32)


def _dot_nt(a, b):
    return lax.dot_general(a, b, (((1,), (1,)), ((), ())), preferred_element_type=F32)


def _dot_tn(a, b):
    return lax.dot_general(a, b, (((0,), (0,)), ((), ())), preferred_element_type=F32)


def _norm_mod(x, g, shift, scale):
    ms = jnp.mean(x * x, axis=-1, keepdims=True)
    y = x * lax.rsqrt(ms + EPS) * g
    return y * (1.0 + scale) + shift


def _row_rms(x, g, n):
    ms = jnp.sum(x * x, axis=-1, keepdims=True) * (1.0 / n)
    return x * lax.rsqrt(ms + EPS) * g


def _group_rms(y, gain, bd, group):
    sq = y * y
    hi = sq.astype(BF16)
    lo = (sq - hi.astype(F32)).astype(BF16)
    ss = _dot(hi, bd) + _dot(lo, bd)
    return y * lax.rsqrt(ss * (1.0 / group) + EPS) * gain


def _pipelined_blocks(project, jobs):
    y_next = project(jobs[0][0])
    for j, (_, ref, col, finish) in enumerate(jobs):
        y = y_next
        if j + 1 < len(jobs):
            y_next = project(jobs[j + 1][0])
        ref[0, :, col:col + MXU_COLS] = finish(y).astype(ref.dtype)


def _rope(y, cos, sin_signed, quarter):
    n = y.shape[-1]
    lane = lax.broadcasted_iota(jnp.int32, (1, n), 1)
    first = (lane & quarter) == 0
    partner = jnp.where(first, pltpu.roll(y, n - quarter, 1), pltpu.roll(y, quarter, 1))
    return y * cos + partner * sin_signed


def _adaln_kernel(cond_ref, w_ref, b_ref, o_ref):
    c = cond_ref[...]
    a = (c * jax.nn.sigmoid(c)).astype(BF16)
    o_ref[0] = _dot(a, w_ref[0].astype(BF16)) + b_ref[0]


def _adaln(cond, w_mod, b_mod):
    depth, d, n = w_mod.shape
    rows = cond.shape[0]
    tn = 1536
    return pl.pallas_call(
        _adaln_kernel,
        out_shape=jax.ShapeDtypeStruct((depth, rows, n), F32),
        grid=(depth, n // tn),
        in_specs=[
            pl.BlockSpec((rows, d), lambda l, j: (0, 0)),
            pl.BlockSpec((1, d, tn), lambda l, j: (l, 0, j)),
            pl.BlockSpec((1, 1, tn), lambda l, j: (l, 0, j)),
        ],
        out_specs=pl.BlockSpec((1, rows, tn), lambda l, j: (l, 0, j)),
        compiler_params=_params("parallel", "parallel"),
        name="adaln",
    )(cond, w_mod, b_mod.reshape(depth, 1, n))


def _mod_spec(mod):
    if mod.shape[0] == 1:
        return pl.BlockSpec((1,) + mod.shape[1:], lambda b, i: (0, 0, 0))
    return pl.BlockSpec((1,) + mod.shape[1:], lambda b, i: (b, 0, 0))


def _qkv_kernel(*refs, nq, nk, rope):
    x_ref, g_ref, mod_ref, w_ref, qg_ref, kg_ref, bd_ref = refs[:7]
    if rope:
        cos, sin = refs[7][...], refs[8][...]
        q_ref, k_ref, v_ref = refs[9:]
    else:
        cos = sin = None
        q_ref, k_ref, v_ref = refs[7:]
    h = _norm_mod(x_ref[0], g_ref[...], mod_ref[0, 0:1, :], mod_ref[0, 1:2, :]).astype(BF16)
    bd = bd_ref[...]

    def head(y, gain):
        y = _group_rms(y, gain, bd, HEAD_DIM)
        return _rope(y, cos, sin, HEAD_DIM // 4) if rope else y

    jobs = [(c, q_ref, c, lambda y: head(y, qg_ref[...])) for c in range(0, nq, MXU_COLS)]
    jobs += [(nq + c, k_ref, c, lambda y: head(y, kg_ref[...])) for c in range(0, nk, MXU_COLS)]
    jobs += [(nq + nk + c, v_ref, c, lambda y: y) for c in range(0, nk, MXU_COLS)]
    _pipelined_blocks(lambda col: _dot(h, w_ref[:, col:col + MXU_COLS]), jobs)


def _qkv_project(x, g, mod, w, q_gain, k_gain, nq, nk, kv_dtype, rope_tables=None):
    b, l, d = x.shape
    tl = min(TOKEN_TILE, l)
    rope = rope_tables is not None
    bd = _block_diag_ones(MXU_COLS, HEAD_DIM)
    args = [x, g.reshape(1, d), mod, w, _tile_gain(q_gain), _tile_gain(k_gain), bd]
    in_specs = [
        pl.BlockSpec((1, tl, d), lambda b_, i: (b_, i, 0)),
        _resident((1, d)),
        _mod_spec(mod),
        _resident(w.shape),
        _resident((1, MXU_COLS)),
        _resident((1, MXU_COLS)),
        _resident(bd.shape),
    ]
    if rope:
        args += list(rope_tables)
        in_specs += [pl.BlockSpec((tl, MXU_COLS), lambda b_, i: (i, 0))] * 2
    out = lambda n, dt: (jax.ShapeDtypeStruct((b, l, n), dt),
                         pl.BlockSpec((1, tl, n), lambda b_, i: (b_, i, 0)))
    shapes, specs = zip(out(nq, BF16), out(nk, kv_dtype), out(nk, kv_dtype))
    return pl.pallas_call(
        functools.partial(_qkv_kernel, nq=nq, nk=nk, rope=rope),
        out_shape=shapes,
        grid=(b, l // tl),
        in_specs=in_specs,
        out_specs=specs,
        compiler_params=_params("parallel", "parallel"),
        name="qkv_project",
    )(*args)


def _mla_project_kernel(*refs, rope):
    (x_ref, g_ref, mod_ref, wdq_ref, qlg_ref, wuq_ref, qng_ref, qpg_ref, bdn_ref, bdp_ref,
     wdkv_ref, kvg_ref, kpg_ref) = refs[:13]
    if rope:
        cos, sin = refs[13][...], refs[14][...]
        qn_ref, qp_ref, ckv_ref, kpe_ref = refs[15:]
    else:
        cos = sin = None
        qn_ref, qp_ref, ckv_ref, kpe_ref = refs[13:]
    h = _norm_mod(x_ref[0], g_ref[...], mod_ref[0, 0:1, :], mod_ref[0, 1:2, :]).astype(BF16)
    q_lora = wdq_ref.shape[1]
    ql = _row_rms(_dot(h, wdq_ref[...]), qlg_ref[...], q_lora).astype(BF16)
    n_nope = qn_ref.shape[2]
    n_pe = qp_ref.shape[2]
    def pe_head(y):
        y = _group_rms(y, qpg_ref[...], bdp_ref[...], MLA_ROPE)
        return _rope(y, cos, sin, MLA_ROPE // 4) if rope else y

    jobs = [(c, qn_ref, c, lambda y: _group_rms(y, qng_ref[...], bdn_ref[...], MLA_NOPE))
            for c in range(0, n_nope, MXU_COLS)]
    jobs += [(n_nope + c, qp_ref, c, pe_head) for c in range(0, n_pe, MXU_COLS)]
    _pipelined_blocks(lambda col: _dot(ql, wuq_ref[:, col:col + MXU_COLS]), jobs)
    ckv = _dot(h, wdkv_ref[:, :MLA_KV_LORA])
    ckv_ref[0] = _row_rms(ckv, kvg_ref[...], MLA_KV_LORA)
    kpe = _row_rms(_dot(h, wdkv_ref[:, MLA_KV_LORA:]), kpg_ref[...], MLA_ROPE)
    if rope:
        kpe = _rope(kpe, cos[:, :LANES], sin[:, :LANES], MLA_ROPE // 4)
    kpe_ref[0] = kpe


def _mla_project(x, g, mod, w_dq, q_lora_g, w_uq, q_nope_g, q_pe_g, w_dkv, kv_lora_g, k_pe_g,
                 rope_tables=None):
    b, l, d = x.shape
    tl = min(TOKEN_TILE, l)
    rope = rope_tables is not None
    n_nope, n_pe = N_HEADS * MLA_NOPE, N_HEADS * MLA_ROPE
    bdn = _block_diag_ones(MXU_COLS, MLA_NOPE)
    bdp = _block_diag_ones(MXU_COLS, MLA_ROPE)
    kpg = jnp.pad(k_pe_g, (0, LANES - MLA_ROPE)).reshape(1, LANES)
    args = [x, g.reshape(1, d), mod, w_dq, q_lora_g.reshape(1, -1), w_uq, _tile_gain(q_nope_g),
            _tile_gain(q_pe_g), bdn, bdp, w_dkv, kv_lora_g.reshape(1, -1), kpg]
    in_specs = [pl.BlockSpec((1, tl, d), lambda b_, i: (b_, i, 0)), _resident((1, d)), _mod_spec(mod)]
    in_specs += [_resident(a.shape) for a in args[3:]]
    if rope:
        args += list(rope_tables)
        in_specs += [pl.BlockSpec((tl, MXU_COLS), lambda b_, i: (i, 0))] * 2
    out = lambda n, dt: (jax.ShapeDtypeStruct((b, l, n), dt),
                         pl.BlockSpec((1, tl, n), lambda b_, i: (b_, i, 0)))
    shapes, specs = zip(out(n_nope, BF16), out(n_pe, BF16), out(MLA_KV_LORA, F32), out(LANES, F32))
    return pl.pallas_call(
        functools.partial(_mla_project_kernel, rope=rope),
        out_shape=shapes,
        grid=(b, l // tl),
        in_specs=in_specs,
        out_specs=specs,
        compiler_params=_params("parallel", "parallel"),
        name="mla_project",
    )(*args)


def _mla_expand_kernel(ckv_ref, kpe_ref, w_ref, kng_ref, bd_ref, tile_ref, kn_ref, v_ref, kpt_ref):
    c = ckv_ref[0].astype(BF16)
    n = kn_ref.shape[2]
    jobs = [(col, kn_ref, col, lambda y: _group_rms(y, kng_ref[...], bd_ref[...], MLA_NOPE))
            for col in range(0, n, MXU_COLS)]
    jobs += [(n + col, v_ref, col, lambda y: y) for col in range(0, n, MXU_COLS)]
    _pipelined_blocks(lambda col: _dot(c, w_ref[:, col:col + MXU_COLS]), jobs)
    kpt_ref[0] = _dot(kpe_ref[0].astype(BF16), tile_ref[...]).astype(kpt_ref.dtype)


def _mla_expand(ckv, kpe, w_ukv, k_nope_g):
    b, l, _ = ckv.shape
    tl = min(TOKEN_TILE, l)
    n = N_HEADS * MLA_NOPE
    pw = kpe.shape[2]
    bd = _block_diag_ones(MXU_COLS, MLA_NOPE)
    tile = np.zeros((pw, LANES), np.float32)
    for r in range(LANES // MLA_ROPE):
        tile[np.arange(MLA_ROPE), r * MLA_ROPE + np.arange(MLA_ROPE)] = 1.0
    tile = jnp.asarray(tile, BF16)
    out = lambda m: (jax.ShapeDtypeStruct((b, l, m), BF16), pl.BlockSpec((1, tl, m), lambda b_, i: (b_, i, 0)))
    shapes, specs = zip(out(n), out(n), out(LANES))
    return pl.pallas_call(
        _mla_expand_kernel,
        out_shape=shapes,
        grid=(b, l // tl),
        in_specs=[
            pl.BlockSpec((1, tl, MLA_KV_LORA), lambda b_, i: (b_, i, 0)),
            pl.BlockSpec((1, tl, pw), lambda b_, i: (b_, i, 0)),
            _resident(w_ukv.shape),
            _resident((1, MXU_COLS)),
            _resident(bd.shape),
            _resident(tile.shape),
        ],
        out_specs=specs,
        compiler_params=_params("parallel", "parallel"),
        name="mla_expand",
    )(ckv, kpe, w_ukv, _tile_gain(k_nope_g), bd, tile)


def _attn_kernel(*refs, mode, ctx, mla, sink, tq, rows):
    it = iter(refs)
    q_ref, k_ref, v_ref = next(it), next(it), next(it)
    kc_ref, vc_ref = (next(it), next(it)) if ctx else (None, None)
    q2_ref, k2_ref = (next(it), next(it)) if mla else (None, None)
    kc2_ref = next(it) if (mla and ctx) else None
    sink_ref = next(it) if sink else None
    tbl_ref = next(it) if mode == "na" else None
    o_ref = next(it)

    p = pl.program_id(1)
    t = pl.program_id(2)
    lane = lax.broadcasted_iota(jnp.int32, (1, LANES), 1)

    def stack(x, m0, m1):
        zero = jnp.zeros_like(x)
        return jnp.concatenate([jnp.where(m0, x, zero), jnp.where(m1, x, zero)], axis=0)

    qs = stack(q_ref[0].astype(BF16), lane < HEAD_DIM, lane >= HEAD_DIM)
    if mla:
        base = (p % 2) * (2 * MLA_ROPE)
        m0 = (lane >= base) & (lane < base + MLA_ROPE)
        m1 = (lane >= base + MLA_ROPE) & (lane < base + 2 * MLA_ROPE)
        qs = jnp.concatenate([qs, stack(q2_ref[0].astype(BF16), m0, m1)], axis=1)

    def keys(ref, ref2, rows_):
        kk = ref[0, rows_, :].astype(BF16)
        if mla:
            kk = jnp.concatenate([kk, ref2[0, rows_, :].astype(BF16)], axis=1)
        return kk

    chunks = []
    if mode == "full":
        lk = k_ref.shape[1]
        kc_len = min(KEY_CHUNK, lk)
        for c in range(0, lk, kc_len):
            rows_ = slice(c, c + kc_len)
            chunks.append((keys(k_ref, k2_ref, rows_), v_ref[0, rows_, :].astype(BF16), None))
    elif mode == "swa":
        win = tq + 2 * SWA_WINDOW
        ws = pl.multiple_of(jnp.clip(t * tq - SWA_WINDOW, 0, k_ref.shape[1] - win), SWA_WINDOW)

        def band(s):
            kpos = ws + lax.broadcasted_iota(jnp.int32, (win, 1), 0)
            qi = t * tq + lax.broadcasted_iota(jnp.int32, (1, tq), 1)
            qpos = jnp.concatenate([qi, qi], axis=1)
            return jnp.where(jnp.abs(kpos - qpos) <= SWA_WINDOW, s, NEG_INF)

        chunks.append((k_ref[0, pl.ds(ws, win), :], v_ref[0, pl.ds(ws, win), :], band))
    else:
        qrows = tq // GRID_W
        wrows = qrows + NA_WIN_ROWS
        r0 = t * qrows
        ws_row = jnp.clip(r0 - NA_WIN_ROWS // 2, 0, rows - wrows)
        ws = pl.multiple_of(ws_row * GRID_W, GRID_W)

        def window_bias(s):
            e0 = ws_row - r0 + (NA_WIN_ROWS - 1) + qrows - 1
            out_rows = []
            for j in range(wrows):
                kr = ws_row + j
                blocks = []
                for hh in range(2):
                    for ii in range(qrows // 2):
                        ok = []
                        for qr in (r0 + 2 * ii, r0 + 2 * ii + 1):
                            rs = jnp.clip(qr - NA_WIN_ROWS // 2, 0, rows - NA_WIN_ROWS)
                            ok.append(((kr >= rs) & (kr < rs + NA_WIN_ROWS)).astype(jnp.int32))
                        valid = jnp.where(lane < GRID_W, ok[0], ok[1]) > 0
                        lo = hh * tq + ii * LANES
                        blk = s[j * GRID_W:(j + 1) * GRID_W, lo:lo + LANES] + tbl_ref[hh, e0 + j - 2 * ii]
                        blocks.append(jnp.where(valid, blk, NEG_INF))
                out_rows.append(jnp.concatenate(blocks, axis=1))
            return jnp.concatenate(out_rows, axis=0)

        chunks.append((k_ref[0, pl.ds(ws, wrows * GRID_W), :], v_ref[0, pl.ds(ws, wrows * GRID_W), :],
                       window_bias))
    if ctx:
        lc = kc_ref.shape[1]
        kc_len = min(KEY_CHUNK, lc)
        for c in range(0, lc, kc_len):
            rows_ = slice(c, c + kc_len)
            chunks.append((keys(kc_ref, kc2_ref, rows_), vc_ref[0, rows_, :].astype(BF16), None))

    if sink:
        first = lax.broadcasted_iota(jnp.int32, (1, 2 * tq), 1) < tq
        sk = jnp.where(first, sink_ref[2 * p], sink_ref[2 * p + 1])

    def scores(chunk):
        kk, _, fix = chunk
        s = _dot_nt(kk, qs)
        return s if fix is None else fix(s)

    m = den = acc = None
    s_next = scores(chunks[0])
    for j, (_, vv, _) in enumerate(chunks):
        s = s_next
        if j + 1 < len(chunks):
            s_next = scores(chunks[j + 1])
        m_c = jnp.max(s, axis=0, keepdims=True)
        if m is None:
            m = jnp.maximum(m_c, sk) if sink else m_c
            pr = jnp.exp2(s - m)
            den = jnp.sum(pr, axis=0, keepdims=True)
            acc = _dot_tn(vv, pr.astype(BF16))
        else:
            m_new = jnp.maximum(m, m_c)
            alpha = jnp.exp2(m - m_new)
            pr = jnp.exp2(s - m_new)
            den = alpha * den + jnp.sum(pr, axis=0, keepdims=True)
            acc = alpha * acc + _dot_tn(vv, pr.astype(BF16))
            m = m_new
    if sink:
        den = den + jnp.exp2(sk - m)
    o = acc / den
    o_pair = jnp.concatenate([o[:HEAD_DIM, :tq], o[HEAD_DIM:, tq:]], axis=0)
    o_ref[0] = o_pair.T.astype(o_ref.dtype)


def _attention(q, k, v, *, mode="full", kc=None, vc=None, q2=None, k2=None, kc2=None, sink=None,
               na_table=None):
    b, lq, nq = q.shape
    lk = k.shape[1]
    pairs = nq // LANES
    group = pairs // (k.shape[2] // LANES)
    tq = min(Q_TILE, lq)
    ctx, mla = kc is not None, q2 is not None

    qmap = lambda b_, p, t: (b_, t, p)
    kvmap = lambda b_, p, t: (b_, 0, p // group)
    args = [q, k, v]
    in_specs = [pl.BlockSpec((1, tq, LANES), qmap),
                pl.BlockSpec((1, lk, LANES), kvmap),
                pl.BlockSpec((1, lk, LANES), kvmap)]
    if ctx:
        lc = kc.shape[1]
        args += [kc, vc]
        in_specs += [pl.BlockSpec((1, lc, LANES), kvmap)] * 2
    if mla:
        args += [q2, k2]
        in_specs += [pl.BlockSpec((1, tq, LANES), lambda b_, p, t: (b_, t, p // 2)),
                     pl.BlockSpec((1, lk, LANES), lambda b_, p, t: (b_, 0, 0))]
        if ctx:
            args += [kc2]
            in_specs += [pl.BlockSpec((1, kc2.shape[1], LANES), lambda b_, p, t: (b_, 0, 0))]
    if sink is not None:
        args += [sink]
        in_specs += [pl.BlockSpec(memory_space=pltpu.SMEM)]
    if mode == "na":
        args += [na_table]
        in_specs += [pl.BlockSpec((2,) + na_table.shape[1:], lambda b_, p, t: (p, 0, 0, 0))]
    return pl.pallas_call(
        functools.partial(_attn_kernel, mode=mode, ctx=ctx, mla=mla, sink=sink is not None, tq=tq,
                          rows=lq // GRID_W),
        out_shape=jax.ShapeDtypeStruct((b, lq, nq), BF16),
        grid=(b, pairs, lq // tq),
        in_specs=in_specs,
        out_specs=pl.BlockSpec((1, tq, LANES), qmap),
        compiler_params=_params("parallel", "parallel", "parallel"),
        name="attention_" + mode,
    )(*args)


def _out_ffn_kernel(x_ref, a_ref, mod_ref, g_ref, wo_ref, wg_ref, wu_ref, wd_ref, o_ref, *, chunks):
    mod = lambda r: mod_ref[0, r:r + 1, :]
    x1 = x_ref[0] + mod(2) * _dot(a_ref[0], wo_ref[...])
    h = _norm_mod(x1, g_ref[...], mod(3), mod(4)).astype(BF16)
    hidden = wg_ref.shape[1]
    step = hidden // chunks
    gate_up = lambda c: (_dot(h, wg_ref[:, c:c + step]), _dot(h, wu_ref[:, c:c + step]))
    y = None
    nxt = gate_up(0)
    for c in range(0, hidden, step):
        gate, up = nxt
        if c + step < hidden:
            nxt = gate_up(c + step)
        act = (gate * jax.nn.sigmoid(gate) * up).astype(BF16)
        part = _dot(act, wd_ref[c:c + step, :])
        y = part if y is None else y + part
    o_ref[0] = x1 + mod(5) * y


def _out_ffn(x, attn, mod, g2, w_o, w_gate, w_up, w_down):
    b, l, d = x.shape
    tl = min(TOKEN_TILE, l)
    tok = lambda n: pl.BlockSpec((1, tl, n), lambda b_, i: (b_, i, 0))
    return pl.pallas_call(
        functools.partial(_out_ffn_kernel, chunks=11),
        out_shape=jax.ShapeDtypeStruct((b, l, d), F32),
        grid=(b, l // tl),
        in_specs=[tok(d), tok(attn.shape[2]), _mod_spec(mod), _resident((1, d)),
                  _resident(w_o.shape), _resident(w_gate.shape), _resident(w_up.shape),
                  _resident(w_down.shape)],
        out_specs=tok(d),
        compiler_params=_params("parallel", "parallel"),
        name="out_ffn",
    )(x, attn, mod, g2.reshape(1, d), w_o, w_gate, w_up, w_down)


def _block_diag_ones(n, group):
    idx = np.arange(n) // group
    return jnp.asarray(idx[:, None] == idx[None, :], BF16)


def _tile_gain(g, scale=1.0):
    return jnp.tile(g * scale, MXU_COLS // g.shape[0]).reshape(1, MXU_COLS)


def _rope_tables(n_tokens, r):
    half, quarter = r // 2, r // 4
    t = jnp.arange(n_tokens, dtype=jnp.int32)
    rows, cols = (t // GRID_W).astype(F32), (t % GRID_W).astype(F32)
    inv_freq = ROPE_THETA ** (-jnp.arange(quarter, dtype=F32) / quarter)
    d = np.arange(MXU_COLS) % r
    use_row = jnp.asarray(d < half)
    freq = inv_freq[d % quarter]
    ang = jnp.where(use_row[None, :], rows[:, None], cols[:, None]) * freq[None, :]
    sign = jnp.asarray(np.where((d % half) < quarter, -1.0, 1.0), F32)
    return jnp.cos(ang), jnp.sin(ang) * sign[None, :]


_GQA_HEADS = np.array([8 * (p // 4) + 4 * e + (p % 4) for p in range(N_HEADS // 2) for e in range(2)])
_GQA_COLS = (_GQA_HEADS[:, None] * HEAD_DIM + np.arange(HEAD_DIM)[None, :]).reshape(-1)


def _na_bias_table(rpb, q_rows):
    qc = np.arange(GRID_W)[None, :]
    kc = np.arange(GRID_W)[:, None]
    start = np.clip(qc - NA_WIN_COLS // 2, 0, GRID_W - NA_WIN_COLS)
    valid = (kc >= start) & (kc < start + NA_WIN_COLS)
    dcol = np.clip(kc - qc + NA_WIN_COLS - 1, 0, 2 * NA_WIN_COLS - 2)
    t = jnp.where(jnp.asarray(valid)[None, None], rpb[:, :, dcol], NEG_INF)
    pad = q_rows
    t = jnp.pad(t, ((0, 0), (pad, pad), (0, 0), (0, 0)))
    n_e = 2 * q_rows + 2 * NA_WIN_ROWS - 2
    first = t[:, 1:1 + n_e]
    second = t[:, 0:n_e]
    return jnp.concatenate([first, second], axis=-1)


def kernel(x_prompt, x_sample, cache_na_k, cache_na_v, cache_swa_k, cache_swa_v, cache_mla_ckv, cache_mla_kpe, cache_gqa_k, cache_gqa_v, c, c_ctx, norm1_g, norm2_g, w_mod, b_mod, na_w_qkv, na_q_g, na_k_g, na_rpb, na_w_o, swa_w_qkv, swa_q_g, swa_k_g, swa_sink, swa_w_o, mla_w_dq, mla_q_lora_g, mla_w_uq, mla_q_nope_g, mla_q_pe_g, mla_w_dkv, mla_kv_lora_g, mla_k_pe_g, mla_w_ukv, mla_k_nope_g, mla_w_o, gqa_w_qkv, gqa_q_g, gqa_k_g, gqa_w_o, ffn_w_gate, ffn_w_up, ffn_w_down):
    depth, d = norm1_g.shape
    bp, lp, _ = x_prompt.shape
    bs, ls, _ = x_sample.shape
    hq, hkv = N_HEADS * HEAD_DIM, N_KV_HEADS * HEAD_DIM
    scale = HEAD_DIM ** -0.5 * LOG2E

    cond = jnp.concatenate([c_ctx[None, :], c, jnp.zeros((16 - 1 - bs, d), F32)], axis=0)
    mods = _adaln(cond, w_mod, b_mod).reshape(depth, 16, 6, d)
    rope64 = _rope_tables(ls, HEAD_DIM)
    rope32 = _rope_tables(ls, MLA_ROPE)
    flat = lambda a, r: a[:, r].reshape(a.shape[0], a.shape[2], -1)

    xp, xs = x_prompt, x_sample
    outs = {}
    for li in range(depth):
        kind, r = li % 4, li // 4
        mp, ms = mods[li, 0:1], mods[li, 1:1 + bs]
        g1 = norm1_g[li]
        if kind in (0, 1, 3):
            w_qkv, q_g, k_g, w_o = ((na_w_qkv, na_q_g, na_k_g, na_w_o), (swa_w_qkv, swa_q_g, swa_k_g, swa_w_o),
                                    None, (gqa_w_qkv, gqa_q_g, gqa_k_g, gqa_w_o))[kind]
            w_qkv, w_o = w_qkv[r], w_o[r]
            nk = hq if kind == 0 else hkv
            if kind != 0:
                w_qkv = jnp.concatenate([w_qkv[:, :hq][:, _GQA_COLS], w_qkv[:, hq:]], axis=1)
                w_o = w_o[_GQA_COLS, :]
            w_qkv, w_o = w_qkv.astype(BF16), w_o.astype(BF16)
            qp, kp, vp = _qkv_project(xp, g1, mp, w_qkv, q_g[r] * scale, k_g[r], hq, nk, F32)
            qs, ks, vs = _qkv_project(xs, g1, ms, w_qkv, q_g[r] * scale, k_g[r], hq, nk, BF16,
                                      rope_tables=None if kind == 0 else rope64)
            if kind == 0:
                kc, vc = flat(cache_na_k, r), flat(cache_na_v, r)
                ap = _attention(qp, kp, vp)
                as_ = _attention(qs, ks, vs, mode="na", kc=kc, vc=vc,
                                 na_table=_na_bias_table(na_rpb[r] * LOG2E, Q_TILE // GRID_W))
                outs["na_k"], outs["na_v"] = kp, vp
            elif kind == 1:
                kc, vc = flat(cache_swa_k, r), flat(cache_swa_v, r)
                sink = swa_sink[r][_GQA_HEADS] * LOG2E
                ap = _attention(qp, kp, vp, sink=sink)
                as_ = _attention(qs, ks, vs, mode="swa", kc=kc, vc=vc, sink=sink)
                outs["swa_k"], outs["swa_v"] = kp, vp
            else:
                kc, vc = flat(cache_gqa_k, r), flat(cache_gqa_v, r)
                ap = _attention(qp, kp, vp)
                as_ = _attention(qs, ks, vs, kc=kc, vc=vc)
                outs["gqa_k"], outs["gqa_v"] = kp, vp
        else:
            dk = MLA_NOPE + MLA_ROPE
            w_uq = mla_w_uq[r].reshape(-1, N_HEADS, dk)
            w_uq = jnp.concatenate([w_uq[:, :, :MLA_NOPE].reshape(-1, N_HEADS * MLA_NOPE),
                                    w_uq[:, :, MLA_NOPE:].reshape(-1, N_HEADS * MLA_ROPE)], axis=1).astype(BF16)
            w_ukv = mla_w_ukv[r].reshape(-1, N_HEADS, MLA_NOPE + MLA_V)
            w_ukv = jnp.concatenate([w_ukv[:, :, :MLA_NOPE].reshape(-1, N_HEADS * MLA_NOPE),
                                     w_ukv[:, :, MLA_NOPE:].reshape(-1, N_HEADS * MLA_V)], axis=1).astype(BF16)
            w_dkv = jnp.pad(mla_w_dkv[r], ((0, 0), (0, 2 * LANES - MLA_KV_LORA - MLA_ROPE))).astype(BF16)
            w_dq, w_o = mla_w_dq[r].astype(BF16), mla_w_o[r].astype(BF16)
            qscale = dk ** -0.5 * LOG2E
            proj = functools.partial(
                _mla_project, w_dq=w_dq, q_lora_g=mla_q_lora_g[r], w_uq=w_uq, q_nope_g=mla_q_nope_g[r] * qscale,
                q_pe_g=mla_q_pe_g[r] * qscale, w_dkv=w_dkv, kv_lora_g=mla_kv_lora_g[r], k_pe_g=mla_k_pe_g[r])
            qn_p, qpe_p, ckv_p, kpe_p = proj(xp, g1, mp)
            qn_s, qpe_s, ckv_s, kpe_s = proj(xs, g1, ms, rope_tables=rope32)
            kn_p, v_p, kpt_p = _mla_expand(ckv_p, kpe_p, w_ukv, mla_k_nope_g[r])
            kn_s, v_s, kpt_s = _mla_expand(ckv_s, kpe_s, w_ukv, mla_k_nope_g[r])
            kn_c, v_c, kpt_c = _mla_expand(cache_mla_ckv[:, r], cache_mla_kpe[:, r], w_ukv, mla_k_nope_g[r])
            ap = _attention(qn_p, kn_p, v_p, q2=qpe_p, k2=kpt_p)
            as_ = _attention(qn_s, kn_s, v_s, kc=kn_c, vc=v_c, q2=qpe_s, k2=kpt_s, kc2=kpt_c)
            outs["mla_ckv"], outs["mla_kpe"] = ckv_p, kpe_p[:, :, :MLA_ROPE]
        wg, wu, wd = ffn_w_gate[li].astype(BF16), ffn_w_up[li].astype(BF16), ffn_w_down[li].astype(BF16)
        xp = _out_ffn(xp, ap, mp, norm2_g[li], w_o, wg, wu, wd)
        xs = _out_ffn(xs, as_, ms, norm2_g[li], w_o, wg, wu, wd)

    heads = lambda a, h: a.reshape(bp, 1, lp, h, HEAD_DIM)
    return (xp, xs, heads(outs["na_k"], N_HEADS), heads(outs["na_v"], N_HEADS),
            heads(outs["swa_k"], N_KV_HEADS), heads(outs["swa_v"], N_KV_HEADS),
            outs["mla_ckv"][:, None], outs["mla_kpe"][:, None],
            heads(outs["gqa_k"], N_KV_HEADS), heads(outs["gqa_v"], N_KV_HEADS))
```

```python
import functools

import numpy as np
import jax
import jax.numpy as jnp
from jax import lax
from jax.experimental import pallas as pl
from jax.experimental.pallas import tpu as pltpu

GRID_W = 64
HEAD_DIM = 64
N_HEADS = 16
N_KV_HEADS = 4
NA_WIN_ROWS = 8
NA_WIN_COLS = 16
SWA_WINDOW = 128
MLA_KV_LORA = 128
MLA_NOPE = 64
MLA_ROPE = 32
MLA_V = 64
ROPE_THETA = 10000.0
EPS = 1e-6
NEG_INF = -1e30

LANES = 128
MXU_COLS = 256
VMEM_LIMIT = 56 * 1024 * 1024
TOKEN_TILE = 512
Q_TILE = 512
KEY_CHUNK = 512
LOG2E = 1.4426950408889634
PROMPT_PAIRS = 4
LATENT_PAIRS = 2
BOUND_SLACK = 1.01
MAX_SCORE_BOUND = 48.0

F32 = jnp.float32
BF16 = jnp.bfloat16


def _params(*sem):
    return pltpu.CompilerParams(dimension_semantics=sem, vmem_limit_bytes=VMEM_LIMIT)


def _resident(shape):
    nd = len(shape)
    return pl.BlockSpec(shape, lambda *_: (0,) * nd, pipeline_mode=pl.Buffered(1))


def _dot(a, b):
    return jnp.dot(a, b, preferred_element_type=F32)


def _dot_nt(a, b):
    return lax.dot_general(a, b, (((1,), (1,)), ((), ())), preferred_element_type=F32)


def _dot_tn(a, b):
    return lax.dot_general(a, b, (((0,), (0,)), ((), ())), preferred_element_type=F32)


def _norm_mod(x, g, shift, scale):
    ms = jnp.mean(x * x, axis=-1, keepdims=True)
    y = x * lax.rsqrt(ms + EPS) * g
    return y * (1.0 + scale) + shift


def _row_rms(x, g, n):
    ms = jnp.sum(x * x, axis=-1, keepdims=True) * (1.0 / n)
    return x * lax.rsqrt(ms + EPS) * g


def _group_rms(y, gain, bd, group):
    sq = y * y
    hi = sq.astype(BF16)
    lo = (sq - hi.astype(F32)).astype(BF16)
    ss = _dot(hi, bd) + _dot(lo, bd)
    return y * lax.rsqrt(ss * (1.0 / group) + EPS) * gain


def _pipelined_blocks(project, jobs):
    y_next = project(jobs[0][0])
    for j, (_, ref, col, finish) in enumerate(jobs):
        y = y_next
        if j + 1 < len(jobs):
            y_next = project(jobs[j + 1][0])
        ref[0, :, col:col + MXU_COLS] = finish(y).astype(ref.dtype)


def _rope(y, cos, sin_signed, quarter):
    n = y.shape[-1]
    lane = lax.broadcasted_iota(jnp.int32, (1, n), 1)
    first = (lane & quarter) == 0
    partner = jnp.where(first, pltpu.roll(y, n - quarter, 1), pltpu.roll(y, quarter, 1))
    return y * cos + partner * sin_signed


def _adaln_kernel(cond_ref, w_ref, b_ref, o_ref):
    c = cond_ref[...]
    a = (c * jax.nn.sigmoid(c)).astype(BF16)
    o_ref[0] = _dot(a, w_ref[0].astype(BF16)) + b_ref[0]


def _adaln(cond, w_mod, b_mod):
    depth, d, n = w_mod.shape
    rows = cond.shape[0]
    tn = 1536
    return pl.pallas_call(
        _adaln_kernel,
        out_shape=jax.ShapeDtypeStruct((depth, rows, n), F32),
        grid=(depth, n // tn),
        in_specs=[
            pl.BlockSpec((rows, d), lambda l, j: (0, 0)),
            pl.BlockSpec((1, d, tn), lambda l, j: (l, 0, j)),
            pl.BlockSpec((1, 1, tn), lambda l, j: (l, 0, j)),
        ],
        out_specs=pl.BlockSpec((1, rows, tn), lambda l, j: (l, 0, j)),
        compiler_params=_params("parallel", "parallel"),
        name="adaln",
    )(cond, w_mod, b_mod.reshape(depth, 1, n))


def _mod_spec(mod):
    if mod.shape[0] == 1:
        return pl.BlockSpec((1,) + mod.shape[1:], lambda b, i: (0, 0, 0))
    return pl.BlockSpec((1,) + mod.shape[1:], lambda b, i: (b, 0, 0))


def _qkv_kernel(*refs, nq, nk, rope):
    x_ref, g_ref, mod_ref, w_ref, qg_ref, kg_ref, bd_ref = refs[:7]
    if rope:
        cos, sin = refs[7][...], refs[8][...]
        q_ref, k_ref, v_ref = refs[9:]
    else:
        cos = sin = None
        q_ref, k_ref, v_ref = refs[7:]
    h = _norm_mod(x_ref[0], g_ref[...], mod_ref[0, 0:1, :], mod_ref[0, 1:2, :]).astype(BF16)
    bd = bd_ref[...]

    def head(y, gain):
        y = _group_rms(y, gain, bd, HEAD_DIM)
        return _rope(y, cos, sin, HEAD_DIM // 4) if rope else y

    jobs = [(c, q_ref, c, lambda y: head(y, qg_ref[...])) for c in range(0, nq, MXU_COLS)]
    jobs += [(nq + c, k_ref, c, lambda y: head(y, kg_ref[...])) for c in range(0, nk, MXU_COLS)]
    jobs += [(nq + nk + c, v_ref, c, lambda y: y) for c in range(0, nk, MXU_COLS)]
    _pipelined_blocks(lambda col: _dot(h, w_ref[:, col:col + MXU_COLS]), jobs)


def _qkv_project(x, g, mod, w, q_gain, k_gain, nq, nk, kv_dtype, rope_tables=None):
    b, l, d = x.shape
    tl = min(TOKEN_TILE, l)
    rope = rope_tables is not None
    bd = _block_diag_ones(MXU_COLS, HEAD_DIM)
    args = [x, g.reshape(1, d), mod, w, _tile_gain(q_gain), _tile_gain(k_gain), bd]
    in_specs = [
        pl.BlockSpec((1, tl, d), lambda b_, i: (b_, i, 0)),
        _resident((1, d)),
        _mod_spec(mod),
        _resident(w.shape),
        _resident((1, MXU_COLS)),
        _resident((1, MXU_COLS)),
        _resident(bd.shape),
    ]
    if rope:
        args += list(rope_tables)
        in_specs += [pl.BlockSpec((tl, MXU_COLS), lambda b_, i: (i, 0))] * 2
    out = lambda n, dt: (jax.ShapeDtypeStruct((b, l, n), dt),
                         pl.BlockSpec((1, tl, n), lambda b_, i: (b_, i, 0)))
    shapes, specs = zip(out(nq, BF16), out(nk, kv_dtype), out(nk, kv_dtype))
    return pl.pallas_call(
        functools.partial(_qkv_kernel, nq=nq, nk=nk, rope=rope),
        out_shape=shapes,
        grid=(b, l // tl),
        in_specs=in_specs,
        out_specs=specs,
        compiler_params=_params("parallel", "parallel"),
        name="qkv_project",
    )(*args)


def _mla_project_kernel(*refs, rope):
    (x_ref, g_ref, mod_ref, wdq_ref, qlg_ref, wuq_ref, qng_ref, qpg_ref, bdn_ref, bdp_ref,
     wdkv_ref, kvg_ref, kpg_ref) = refs[:13]
    if rope:
        cos, sin = refs[13][...], refs[14][...]
        qn_ref, qp_ref, ckv_ref, kpe_ref = refs[15:]
    else:
        cos = sin = None
        qn_ref, qp_ref, ckv_ref, kpe_ref = refs[13:]
    h = _norm_mod(x_ref[0], g_ref[...], mod_ref[0, 0:1, :], mod_ref[0, 1:2, :]).astype(BF16)
    q_lora = wdq_ref.shape[1]
    ql = _row_rms(_dot(h, wdq_ref[...]), qlg_ref[...], q_lora).astype(BF16)
    n_nope = qn_ref.shape[2]
    n_pe = qp_ref.shape[2]
    def pe_head(y):
        y = _group_rms(y, qpg_ref[...], bdp_ref[...], MLA_ROPE)
        return _rope(y, cos, sin, MLA_ROPE // 4) if rope else y

    jobs = [(c, qn_ref, c, lambda y: _group_rms(y, qng_ref[...], bdn_ref[...], MLA_NOPE))
            for c in range(0, n_nope, MXU_COLS)]
    jobs += [(n_nope + c, qp_ref, c, pe_head) for c in range(0, n_pe, MXU_COLS)]
    _pipelined_blocks(lambda col: _dot(ql, wuq_ref[:, col:col + MXU_COLS]), jobs)
    ckv = _dot(h, wdkv_ref[:, :MLA_KV_LORA])
    ckv_ref[0] = _row_rms(ckv, kvg_ref[...], MLA_KV_LORA)
    kpe = _row_rms(_dot(h, wdkv_ref[:, MLA_KV_LORA:]), kpg_ref[...], MLA_ROPE)
    if rope:
        kpe = _rope(kpe, cos[:, :LANES], sin[:, :LANES], MLA_ROPE // 4)
    kpe_ref[0] = kpe


def _mla_project(x, g, mod, w_dq, q_lora_g, w_uq, q_nope_g, q_pe_g, w_dkv, kv_lora_g, k_pe_g,
                 rope_tables=None):
    b, l, d = x.shape
    tl = min(TOKEN_TILE, l)
    rope = rope_tables is not None
    n_nope, n_pe = N_HEADS * MLA_NOPE, N_HEADS * MLA_ROPE
    bdn = _block_diag_ones(MXU_COLS, MLA_NOPE)
    bdp = _block_diag_ones(MXU_COLS, MLA_ROPE)
    kpg = jnp.pad(k_pe_g, (0, LANES - MLA_ROPE)).reshape(1, LANES)
    args = [x, g.reshape(1, d), mod, w_dq, q_lora_g.reshape(1, -1), w_uq, _tile_gain(q_nope_g),
            _tile_gain(q_pe_g), bdn, bdp, w_dkv, kv_lora_g.reshape(1, -1), kpg]
    in_specs = [pl.BlockSpec((1, tl, d), lambda b_, i: (b_, i, 0)), _resident((1, d)), _mod_spec(mod)]
    in_specs += [_resident(a.shape) for a in args[3:]]
    if rope:
        args += list(rope_tables)
        in_specs += [pl.BlockSpec((tl, MXU_COLS), lambda b_, i: (i, 0))] * 2
    out = lambda n, dt: (jax.ShapeDtypeStruct((b, l, n), dt),
                         pl.BlockSpec((1, tl, n), lambda b_, i: (b_, i, 0)))
    shapes, specs = zip(out(n_nope, BF16), out(n_pe, BF16), out(MLA_KV_LORA, F32), out(LANES, F32))
    return pl.pallas_call(
        functools.partial(_mla_project_kernel, rope=rope),
        out_shape=shapes,
        grid=(b, l // tl),
        in_specs=in_specs,
        out_specs=specs,
        compiler_params=_params("parallel", "parallel"),
        name="mla_project",
    )(*args)


def _mla_expand_kernel(ckv_ref, kpe_ref, w_ref, kng_ref, bd_ref, tile_ref, kn_ref, v_ref, kpt_ref):
    c = ckv_ref[0].astype(BF16)
    n = kn_ref.shape[2]
    jobs = [(col, kn_ref, col, lambda y: _group_rms(y, kng_ref[...], bd_ref[...], MLA_NOPE))
            for col in range(0, n, MXU_COLS)]
    jobs += [(n + col, v_ref, col, lambda y: y) for col in range(0, n, MXU_COLS)]
    _pipelined_blocks(lambda col: _dot(c, w_ref[:, col:col + MXU_COLS]), jobs)
    kpt_ref[0] = _dot(kpe_ref[0].astype(BF16), tile_ref[...]).astype(kpt_ref.dtype)


def _mla_expand(ckv, kpe, w_ukv, k_nope_g):
    b, l, _ = ckv.shape
    tl = min(TOKEN_TILE, l)
    n = N_HEADS * MLA_NOPE
    pw = kpe.shape[2]
    bd = _block_diag_ones(MXU_COLS, MLA_NOPE)
    tile = np.zeros((pw, LANES), np.float32)
    for r in range(LANES // MLA_ROPE):
        tile[np.arange(MLA_ROPE), r * MLA_ROPE + np.arange(MLA_ROPE)] = 1.0
    tile = jnp.asarray(tile, BF16)
    out = lambda m: (jax.ShapeDtypeStruct((b, l, m), BF16), pl.BlockSpec((1, tl, m), lambda b_, i: (b_, i, 0)))
    shapes, specs = zip(out(n), out(n), out(LANES))
    return pl.pallas_call(
        _mla_expand_kernel,
        out_shape=shapes,
        grid=(b, l // tl),
        in_specs=[
            pl.BlockSpec((1, tl, MLA_KV_LORA), lambda b_, i: (b_, i, 0)),
            pl.BlockSpec((1, tl, pw), lambda b_, i: (b_, i, 0)),
            _resident(w_ukv.shape),
            _resident((1, MXU_COLS)),
            _resident(bd.shape),
            _resident(tile.shape),
        ],
        out_specs=specs,
        compiler_params=_params("parallel", "parallel"),
        name="mla_expand",
    )(ckv, kpe, w_ukv, _tile_gain(k_nope_g), bd, tile)


def _attn_kernel(*refs, mode, ctx, mla, sink, tq, rows, pp, group, bounded):
    it = iter(refs)
    q_ref, k_ref, v_ref = next(it), next(it), next(it)
    kc_ref, vc_ref = (next(it), next(it)) if ctx else (None, None)
    q2_ref, k2_ref = (next(it), next(it)) if mla else (None, None)
    kc2_ref = next(it) if (mla and ctx) else None
    sink_ref = next(it) if sink else None
    tbl_ref = next(it) if mode == "na" else None
    bound_ref = next(it) if bounded else None
    o_ref = next(it)

    g = pl.program_id(1)
    t = pl.program_id(2)
    lane = lax.broadcasted_iota(jnp.int32, (1, LANES), 1)
    plans = [_attn_pair_plan(i, g, t, lane, q_ref, k_ref, v_ref, kc_ref, vc_ref, q2_ref, k2_ref, kc2_ref,
                             sink_ref, tbl_ref, mode=mode, ctx=ctx, mla=mla, sink=sink, tq=tq, rows=rows,
                             pp=pp, group=group)
             for i in range(pp)]
    floor = bound_ref[0] if bounded else None

    def scores(i, j):
        kk, _, fix, _ = plans[i][1][j]
        s = _dot_nt(kk, plans[i][0])
        return s if fix is None else fix(s)

    n_chunks = len(plans[0][1])
    state = [None] * pp
    s_next = [scores(i, 0) for i in range(pp)]
    for j in range(n_chunks):
        for i in range(pp):
            s = s_next[i]
            if j + 1 < n_chunks:
                s_next[i] = scores(i, j + 1)
            _, vv, _, exact_max = plans[i][1][j]
            state[i] = _softmax_step(state[i], s, vv, plans[i][2], floor, exact_max or not bounded)
    for i in range(pp):
        m, den, acc = state[i]
        if sink:
            den = den + jnp.exp2(plans[i][2] - m)
        o = acc / den
        o_pair = jnp.concatenate([o[:HEAD_DIM, :tq], o[HEAD_DIM:, tq:]], axis=0)
        o_ref[0, :, i * LANES:(i + 1) * LANES] = o_pair.T.astype(o_ref.dtype)


def _softmax_step(state, s, vv, sk, floor, exact_max):
    def probs(m):
        pr = jnp.exp2(s - m)
        return pr.astype(BF16), jnp.sum(pr, axis=0, keepdims=True)

    if state is None:
        m = None
        if floor is not None:
            m = jnp.full((1, s.shape[1]), floor, F32)
        if sk is not None:
            m = sk if m is None else jnp.maximum(m, sk)
        if exact_max:
            m_c = jnp.max(s, axis=0, keepdims=True)
            m = m_c if m is None else jnp.maximum(m, m_c)
        pr, total = probs(m)
        return m, total, _dot_tn(vv, pr)
    m, den, acc = state
    if exact_max:
        m_new = jnp.maximum(m, jnp.max(s, axis=0, keepdims=True))
        alpha = jnp.exp2(m - m_new)
        pr, total = probs(m_new)
        return m_new, alpha * den + total, alpha * acc + _dot_tn(vv, pr)
    pr, total = probs(m)
    return m, den + total, acc + _dot_tn(vv, pr)


def _attn_pair_plan(i, g, t, lane, q_ref, k_ref, v_ref, kc_ref, vc_ref, q2_ref, k2_ref, kc2_ref, sink_ref,
                    tbl_ref, *, mode, ctx, mla, sink, tq, rows, pp, group):
    ql = slice(i * LANES, (i + 1) * LANES)
    kvi = i // group if pp >= group else 0
    kl = slice(kvi * LANES, (kvi + 1) * LANES)

    def stack(x, m0, m1):
        zero = jnp.zeros_like(x)
        return jnp.concatenate([jnp.where(m0, x, zero), jnp.where(m1, x, zero)], axis=0)

    qs = stack(q_ref[0, :, ql].astype(BF16), lane < HEAD_DIM, lane >= HEAD_DIM)
    if mla:
        if pp == 1:
            q2, base = q2_ref[0], (g % 2) * (2 * MLA_ROPE)
        else:
            q2, base = q2_ref[0, :, (i // 2) * LANES:(i // 2 + 1) * LANES], (i % 2) * (2 * MLA_ROPE)
        m0 = (lane >= base) & (lane < base + MLA_ROPE)
        m1 = (lane >= base + MLA_ROPE) & (lane < base + 2 * MLA_ROPE)
        qs = jnp.concatenate([qs, stack(q2.astype(BF16), m0, m1)], axis=1)

    def keys(ref, ref2, rows_):
        kk = ref[0, rows_, kl].astype(BF16)
        if mla:
            kk = jnp.concatenate([kk, ref2[0, rows_, :].astype(BF16)], axis=1)
        return kk

    chunks = []
    if ctx:
        lc = kc_ref.shape[1]
        kc_len = min(KEY_CHUNK, lc)
        for c in range(0, lc, kc_len):
            rows_ = slice(c, c + kc_len)
            chunks.append((keys(kc_ref, kc2_ref, rows_), vc_ref[0, rows_, kl].astype(BF16), None, True))
    if mode == "full":
        lk = k_ref.shape[1]
        kc_len = min(KEY_CHUNK, lk)
        for c in range(0, lk, kc_len):
            rows_ = slice(c, c + kc_len)
            chunks.append((keys(k_ref, k2_ref, rows_), v_ref[0, rows_, kl].astype(BF16), None, False))
    elif mode == "swa":
        win = tq + 2 * SWA_WINDOW
        ws = pl.multiple_of(jnp.clip(t * tq - SWA_WINDOW, 0, k_ref.shape[1] - win), SWA_WINDOW)

        def band(s):
            kpos = ws + lax.broadcasted_iota(jnp.int32, (win, 1), 0)
            qi = t * tq + lax.broadcasted_iota(jnp.int32, (1, tq), 1)
            qpos = jnp.concatenate([qi, qi], axis=1)
            return jnp.where(jnp.abs(kpos - qpos) <= SWA_WINDOW, s, NEG_INF)

        chunks.append((k_ref[0, pl.ds(ws, win), kl], v_ref[0, pl.ds(ws, win), kl], band, False))
    else:
        qrows = tq // GRID_W
        wrows = qrows + NA_WIN_ROWS
        r0 = t * qrows
        ws_row = jnp.clip(r0 - NA_WIN_ROWS // 2, 0, rows - wrows)
        ws = pl.multiple_of(ws_row * GRID_W, GRID_W)

        def window_bias(s):
            e0 = ws_row - r0 + (NA_WIN_ROWS - 1) + qrows - 1
            out_rows = []
            for j in range(wrows):
                kr = ws_row + j
                blocks = []
                for hh in range(2):
                    for ii in range(qrows // 2):
                        ok = []
                        for qr in (r0 + 2 * ii, r0 + 2 * ii + 1):
                            rs = jnp.clip(qr - NA_WIN_ROWS // 2, 0, rows - NA_WIN_ROWS)
                            ok.append(((kr >= rs) & (kr < rs + NA_WIN_ROWS)).astype(jnp.int32))
                        valid = jnp.where(lane < GRID_W, ok[0], ok[1]) > 0
                        lo = hh * tq + ii * LANES
                        blk = (s[j * GRID_W:(j + 1) * GRID_W, lo:lo + LANES]
                               + tbl_ref[2 * i + hh, e0 + j - 2 * ii])
                        blocks.append(jnp.where(valid, blk, NEG_INF))
                out_rows.append(jnp.concatenate(blocks, axis=1))
            return jnp.concatenate(out_rows, axis=0)

        chunks.append((k_ref[0, pl.ds(ws, wrows * GRID_W), kl], v_ref[0, pl.ds(ws, wrows * GRID_W), kl],
                       window_bias, False))

    sk = None
    if sink:
        first = lax.broadcasted_iota(jnp.int32, (1, 2 * tq), 1) < tq
        head = 2 * (g * pp + i)
        sk = jnp.where(first, sink_ref[head], sink_ref[head + 1])
    return qs, chunks, sk


def _attention(q, k, v, *, mode="full", kc=None, vc=None, q2=None, k2=None, kc2=None, sink=None,
               na_table=None, bound=None, pp=1):
    b, lq, nq = q.shape
    lk = k.shape[1]
    pairs = nq // LANES
    group = pairs // (k.shape[2] // LANES)
    tq = min(Q_TILE, lq)
    ctx, mla = kc is not None, q2 is not None
    assert pairs % pp == 0 and (pp % group == 0 or group % pp == 0)
    assert not mla or pp == 1 or pp % 2 == 0

    kv_lanes = LANES * max(1, pp // group)
    qmap = lambda b_, g, t: (b_, t, g)
    if pp >= group:
        kvmap = lambda b_, g, t: (b_, 0, g)
    else:
        kvmap = lambda b_, g, t: (b_, 0, (g * pp) // group)
    shared = lambda b_, g, t: (b_, 0, 0)
    args = [q, k, v]
    in_specs = [pl.BlockSpec((1, tq, pp * LANES), qmap),
                pl.BlockSpec((1, lk, kv_lanes), kvmap),
                pl.BlockSpec((1, lk, kv_lanes), kvmap)]
    if ctx:
        lc = kc.shape[1]
        args += [kc, vc]
        in_specs += [pl.BlockSpec((1, lc, kv_lanes), kvmap)] * 2
    if mla:
        args += [q2, k2]
        if pp == 1:
            in_specs += [pl.BlockSpec((1, tq, LANES), lambda b_, g, t: (b_, t, g // 2))]
        else:
            in_specs += [pl.BlockSpec((1, tq, pp * LANES // 2), qmap)]
        in_specs += [pl.BlockSpec((1, lk, LANES), shared)]
        if ctx:
            args += [kc2]
            in_specs += [pl.BlockSpec((1, kc2.shape[1], LANES), shared)]
    if sink is not None:
        args += [sink]
        in_specs += [pl.BlockSpec(memory_space=pltpu.SMEM)]
    if mode == "na":
        args += [na_table]
        in_specs += [pl.BlockSpec((2 * pp,) + na_table.shape[1:], lambda b_, g, t: (g, 0, 0, 0))]
    if bound is not None:
        args += [bound]
        in_specs += [pl.BlockSpec(memory_space=pltpu.SMEM)]
    return pl.pallas_call(
        functools.partial(_attn_kernel, mode=mode, ctx=ctx, mla=mla, sink=sink is not None, tq=tq,
                          rows=lq // GRID_W, pp=pp, group=group, bounded=bound is not None),
        out_shape=jax.ShapeDtypeStruct((b, lq, nq), BF16),
        grid=(b, pairs // pp, lq // tq),
        in_specs=in_specs,
        out_specs=pl.BlockSpec((1, tq, pp * LANES), qmap),
        compiler_params=_params("parallel", "parallel", "parallel"),
        name="attention_" + mode,
    )(*args)


def _attention_guarded(score_bound, *args, **kwargs):
    bound = jnp.reshape(score_bound * BOUND_SLACK, (1,)).astype(F32)
    return lax.cond(bound[0] <= MAX_SCORE_BOUND,
                    lambda: _attention(*args, bound=bound, **kwargs),
                    lambda: _attention(*args, **kwargs))


def _out_ffn_kernel(x_ref, a_ref, mod_ref, g_ref, wo_ref, wg_ref, wu_ref, wd_ref, o_ref, *, chunks):
    mod = lambda r: mod_ref[0, r:r + 1, :]
    x1 = x_ref[0] + mod(2) * _dot(a_ref[0], wo_ref[...])
    h = _norm_mod(x1, g_ref[...], mod(3), mod(4)).astype(BF16)
    hidden = wg_ref.shape[1]
    step = hidden // chunks
    gate_up = lambda c: (_dot(h, wg_ref[:, c:c + step]), _dot(h, wu_ref[:, c:c + step]))
    y = None
    nxt = gate_up(0)
    for c in range(0, hidden, step):
        gate, up = nxt
        if c + step < hidden:
            nxt = gate_up(c + step)
        act = (gate * jax.nn.sigmoid(gate) * up).astype(BF16)
        part = _dot(act, wd_ref[c:c + step, :])
        y = part if y is None else y + part
    o_ref[0] = x1 + mod(5) * y


def _out_ffn(x, attn, mod, g2, w_o, w_gate, w_up, w_down):
    b, l, d = x.shape
    tl = min(TOKEN_TILE, l)
    tok = lambda n: pl.BlockSpec((1, tl, n), lambda b_, i: (b_, i, 0))
    return pl.pallas_call(
        functools.partial(_out_ffn_kernel, chunks=11),
        out_shape=jax.ShapeDtypeStruct((b, l, d), F32),
        grid=(b, l // tl),
        in_specs=[tok(d), tok(attn.shape[2]), _mod_spec(mod), _resident((1, d)),
                  _resident(w_o.shape), _resident(w_gate.shape), _resident(w_up.shape),
                  _resident(w_down.shape)],
        out_specs=tok(d),
        compiler_params=_params("parallel", "parallel"),
        name="out_ffn",
    )(x, attn, mod, g2.reshape(1, d), w_o, w_gate, w_up, w_down)


def _max_abs(a):
    return jnp.max(jnp.abs(a))


def _block_diag_ones(n, group):
    idx = np.arange(n) // group
    return jnp.asarray(idx[:, None] == idx[None, :], BF16)


def _tile_gain(g, scale=1.0):
    return jnp.tile(g * scale, MXU_COLS // g.shape[0]).reshape(1, MXU_COLS)


def _rope_tables(n_tokens, r):
    half, quarter = r // 2, r // 4
    t = jnp.arange(n_tokens, dtype=jnp.int32)
    rows, cols = (t // GRID_W).astype(F32), (t % GRID_W).astype(F32)
    inv_freq = ROPE_THETA ** (-jnp.arange(quarter, dtype=F32) / quarter)
    d = np.arange(MXU_COLS) % r
    use_row = jnp.asarray(d < half)
    freq = inv_freq[d % quarter]
    ang = jnp.where(use_row[None, :], rows[:, None], cols[:, None]) * freq[None, :]
    sign = jnp.asarray(np.where((d % half) < quarter, -1.0, 1.0), F32)
    return jnp.cos(ang), jnp.sin(ang) * sign[None, :]


_GQA_HEADS = np.array([8 * (p // 4) + 4 * e + (p % 4) for p in range(N_HEADS // 2) for e in range(2)])
_GQA_COLS = (_GQA_HEADS[:, None] * HEAD_DIM + np.arange(HEAD_DIM)[None, :]).reshape(-1)


def _na_bias_table(rpb, q_rows):
    qc = np.arange(GRID_W)[None, :]
    kc = np.arange(GRID_W)[:, None]
    start = np.clip(qc - NA_WIN_COLS // 2, 0, GRID_W - NA_WIN_COLS)
    valid = (kc >= start) & (kc < start + NA_WIN_COLS)
    dcol = np.clip(kc - qc + NA_WIN_COLS - 1, 0, 2 * NA_WIN_COLS - 2)
    t = jnp.where(jnp.asarray(valid)[None, None], rpb[:, :, dcol], NEG_INF)
    pad = q_rows
    t = jnp.pad(t, ((0, 0), (pad, pad), (0, 0), (0, 0)))
    n_e = 2 * q_rows + 2 * NA_WIN_ROWS - 2
    first = t[:, 1:1 + n_e]
    second = t[:, 0:n_e]
    return jnp.concatenate([first, second], axis=-1)


def kernel(x_prompt, x_sample, cache_na_k, cache_na_v, cache_swa_k, cache_swa_v, cache_mla_ckv, cache_mla_kpe, cache_gqa_k, cache_gqa_v, c, c_ctx, norm1_g, norm2_g, w_mod, b_mod, na_w_qkv, na_q_g, na_k_g, na_rpb, na_w_o, swa_w_qkv, swa_q_g, swa_k_g, swa_sink, swa_w_o, mla_w_dq, mla_q_lora_g, mla_w_uq, mla_q_nope_g, mla_q_pe_g, mla_w_dkv, mla_kv_lora_g, mla_k_pe_g, mla_w_ukv, mla_k_nope_g, mla_w_o, gqa_w_qkv, gqa_q_g, gqa_k_g, gqa_w_o, ffn_w_gate, ffn_w_up, ffn_w_down):
    depth, d = norm1_g.shape
    bp, lp, _ = x_prompt.shape
    bs, ls, _ = x_sample.shape
    hq, hkv = N_HEADS * HEAD_DIM, N_KV_HEADS * HEAD_DIM
    scale = HEAD_DIM ** -0.5 * LOG2E

    cond = jnp.concatenate([c_ctx[None, :], c, jnp.zeros((16 - 1 - bs, d), F32)], axis=0)
    mods = _adaln(cond, w_mod, b_mod).reshape(depth, 16, 6, d)
    rope64 = _rope_tables(ls, HEAD_DIM)
    rope32 = _rope_tables(ls, MLA_ROPE)
    flat = lambda a, r: a[:, r].reshape(a.shape[0], a.shape[2], -1)

    xp, xs = x_prompt, x_sample
    outs = {}
    for li in range(depth):
        kind, r = li % 4, li // 4
        mp, ms = mods[li, 0:1], mods[li, 1:1 + bs]
        g1 = norm1_g[li]
        if kind in (0, 1, 3):
            w_qkv, q_g, k_g, w_o = ((na_w_qkv, na_q_g, na_k_g, na_w_o), (swa_w_qkv, swa_q_g, swa_k_g, swa_w_o),
                                    None, (gqa_w_qkv, gqa_q_g, gqa_k_g, gqa_w_o))[kind]
            w_qkv, w_o = w_qkv[r], w_o[r]
            nk = hq if kind == 0 else hkv
            if kind != 0:
                w_qkv = jnp.concatenate([w_qkv[:, :hq][:, _GQA_COLS], w_qkv[:, hq:]], axis=1)
                w_o = w_o[_GQA_COLS, :]
            w_qkv, w_o = w_qkv.astype(BF16), w_o.astype(BF16)
            qp, kp, vp = _qkv_project(xp, g1, mp, w_qkv, q_g[r] * scale, k_g[r], hq, nk, F32)
            qs, ks, vs = _qkv_project(xs, g1, ms, w_qkv, q_g[r] * scale, k_g[r], hq, nk, BF16,
                                      rope_tables=None if kind == 0 else rope64)
            qk_bound = HEAD_DIM * _max_abs(q_g[r] * scale) * _max_abs(k_g[r])
            if kind == 0:
                kc, vc = flat(cache_na_k, r), flat(cache_na_v, r)
                rpb = na_rpb[r] * LOG2E
                ap = _attention_guarded(qk_bound, qp, kp, vp, pp=PROMPT_PAIRS)
                as_ = _attention_guarded(qk_bound + _max_abs(rpb), qs, ks, vs, mode="na", kc=kc, vc=vc,
                                         na_table=_na_bias_table(rpb, Q_TILE // GRID_W), pp=LATENT_PAIRS)
                outs["na_k"], outs["na_v"] = kp, vp
            elif kind == 1:
                kc, vc = flat(cache_swa_k, r), flat(cache_swa_v, r)
                sink = swa_sink[r][_GQA_HEADS] * LOG2E
                ap = _attention_guarded(qk_bound, qp, kp, vp, sink=sink, pp=PROMPT_PAIRS)
                as_ = _attention_guarded(qk_bound, qs, ks, vs, mode="swa", kc=kc, vc=vc, sink=sink,
                                         pp=LATENT_PAIRS)
                outs["swa_k"], outs["swa_v"] = kp, vp
            else:
                kc, vc = flat(cache_gqa_k, r), flat(cache_gqa_v, r)
                ap = _attention_guarded(qk_bound, qp, kp, vp, pp=PROMPT_PAIRS)
                as_ = _attention_guarded(qk_bound, qs, ks, vs, kc=kc, vc=vc, pp=LATENT_PAIRS)
                outs["gqa_k"], outs["gqa_v"] = kp, vp
        else:
            dk = MLA_NOPE + MLA_ROPE
            w_uq = mla_w_uq[r].reshape(-1, N_HEADS, dk)
            w_uq = jnp.concatenate([w_uq[:, :, :MLA_NOPE].reshape(-1, N_HEADS * MLA_NOPE),
                                    w_uq[:, :, MLA_NOPE:].reshape(-1, N_HEADS * MLA_ROPE)], axis=1).astype(BF16)
            w_ukv = mla_w_ukv[r].reshape(-1, N_HEADS, MLA_NOPE + MLA_V)
            w_ukv = jnp.concatenate([w_ukv[:, :, :MLA_NOPE].reshape(-1, N_HEADS * MLA_NOPE),
                                     w_ukv[:, :, MLA_NOPE:].reshape(-1, N_HEADS * MLA_V)], axis=1).astype(BF16)
            w_dkv = jnp.pad(mla_w_dkv[r], ((0, 0), (0, 2 * LANES - MLA_KV_LORA - MLA_ROPE))).astype(BF16)
            w_dq, w_o = mla_w_dq[r].astype(BF16), mla_w_o[r].astype(BF16)
            qscale = dk ** -0.5 * LOG2E
            proj = functools.partial(
                _mla_project, w_dq=w_dq, q_lora_g=mla_q_lora_g[r], w_uq=w_uq, q_nope_g=mla_q_nope_g[r] * qscale,
                q_pe_g=mla_q_pe_g[r] * qscale, w_dkv=w_dkv, kv_lora_g=mla_kv_lora_g[r], k_pe_g=mla_k_pe_g[r])
            qn_p, qpe_p, ckv_p, kpe_p = proj(xp, g1, mp)
            qn_s, qpe_s, ckv_s, kpe_s = proj(xs, g1, ms, rope_tables=rope32)
            kn_p, v_p, kpt_p = _mla_expand(ckv_p, kpe_p, w_ukv, mla_k_nope_g[r])
            kn_s, v_s, kpt_s = _mla_expand(ckv_s, kpe_s, w_ukv, mla_k_nope_g[r])
            kn_c, v_c, kpt_c = _mla_expand(cache_mla_ckv[:, r], cache_mla_kpe[:, r], w_ukv, mla_k_nope_g[r])
            q_norm = jnp.sqrt(MLA_NOPE * _max_abs(mla_q_nope_g[r] * qscale) ** 2
                              + MLA_ROPE * _max_abs(mla_q_pe_g[r] * qscale) ** 2)
            k_norm = jnp.sqrt(MLA_NOPE * _max_abs(mla_k_nope_g[r]) ** 2 + MLA_ROPE * _max_abs(mla_k_pe_g[r]) ** 2)
            ap = _attention_guarded(q_norm * k_norm, qn_p, kn_p, v_p, q2=qpe_p, k2=kpt_p, pp=PROMPT_PAIRS)
            as_ = _attention_guarded(q_norm * k_norm, qn_s, kn_s, v_s, kc=kn_c, vc=v_c, q2=qpe_s, k2=kpt_s,
                                     kc2=kpt_c, pp=LATENT_PAIRS)
            outs["mla_ckv"], outs["mla_kpe"] = ckv_p, kpe_p[:, :, :MLA_ROPE]
        wg, wu, wd = ffn_w_gate[li].astype(BF16), ffn_w_up[li].astype(BF16), ffn_w_down[li].astype(BF16)
        xp = _out_ffn(xp, ap, mp, norm2_g[li], w_o, wg, wu, wd)
        xs = _out_ffn(xs, as_, ms, norm2_g[li], w_o, wg, wu, wd)

    heads = lambda a, h: a.reshape(bp, 1, lp, h, HEAD_DIM)
    return (xp, xs, heads(outs["na_k"], N_HEADS), heads(outs["na_v"], N_HEADS),
            heads(outs["swa_k"], N_KV_HEADS), heads(outs["swa_v"], N_KV_HEADS),
            outs["mla_ckv"][:, None], outs["mla_kpe"][:, None],
            heads(outs["gqa_k"], N_KV_HEADS), heads(outs["gqa_v"], N_KV_HEADS))
```

```python
import functools

import numpy as np
import jax
import jax.numpy as jnp
from jax import lax
from jax.experimental import pallas as pl
from jax.experimental.pallas import tpu as pltpu

GRID_W = 64
HEAD_DIM = 64
N_HEADS = 16
N_KV_HEADS = 4
NA_WIN_ROWS = 8
NA_WIN_COLS = 16
SWA_WINDOW = 128
MLA_KV_LORA = 128
MLA_NOPE = 64
MLA_ROPE = 32
MLA_V = 64
ROPE_THETA = 10000.0
EPS = 1e-6
NEG_INF = -1e30

LANES = 128
MXU_COLS = 256
VMEM_LIMIT = 56 * 1024 * 1024
TOKEN_TILE = 512
Q_TILE = 512
KEY_CHUNK = 512
LOG2E = 1.4426950408889634
PROMPT_PAIRS = 8
LATENT_PAIRS = 2
WINDOW_Q_TILE = 128
WINDOW_PAIRS = 8
BOUND_SLACK = 1.01
MAX_SCORE_BOUND = 48.0

F32 = jnp.float32
BF16 = jnp.bfloat16


def _params(*sem):
    return pltpu.CompilerParams(dimension_semantics=sem, vmem_limit_bytes=VMEM_LIMIT)


def _resident(shape):
    nd = len(shape)
    return pl.BlockSpec(shape, lambda *_: (0,) * nd, pipeline_mode=pl.Buffered(1))


def _dot(a, b):
    return jnp.dot(a, b, preferred_element_type=F32)


def _dot_nt(a, b):
    return lax.dot_general(a, b, (((1,), (1,)), ((), ())), preferred_element_type=F32)


def _dot_tn(a, b):
    return lax.dot_general(a, b, (((0,), (0,)), ((), ())), preferred_element_type=F32)


def _norm_mod(x, g, shift, scale):
    ms = jnp.mean(x * x, axis=-1, keepdims=True)
    y = x * lax.rsqrt(ms + EPS) * g
    return y * (1.0 + scale) + shift


def _row_rms(x, g, n):
    ms = jnp.sum(x * x, axis=-1, keepdims=True) * (1.0 / n)
    return x * lax.rsqrt(ms + EPS) * g


def _group_rms(y, gain, bd, group):
    ss = _dot((y * y).astype(BF16), bd)
    return y * lax.rsqrt(ss * (1.0 / group) + EPS) * gain


def _pipelined_blocks(project, jobs):
    y_next = project(jobs[0][0])
    for j, (_, ref, col, finish) in enumerate(jobs):
        y = y_next
        if j + 1 < len(jobs):
            y_next = project(jobs[j + 1][0])
        ref[0, :, col:col + MXU_COLS] = finish(y).astype(ref.dtype)


def _rope(y, cos, sin_signed, quarter):
    n = y.shape[-1]
    lane = lax.broadcasted_iota(jnp.int32, (1, n), 1)
    first = (lane & quarter) == 0
    partner = jnp.where(first, pltpu.roll(y, n - quarter, 1), pltpu.roll(y, quarter, 1))
    return y * cos + partner * sin_signed


def _adaln_kernel(cond_ref, w_ref, b_ref, o_ref):
    c = cond_ref[...]
    a = (c * jax.nn.sigmoid(c)).astype(BF16)
    o_ref[0] = _dot(a, w_ref[0].astype(BF16)) + b_ref[0]


def _adaln(cond, w_mod, b_mod):
    depth, d, n = w_mod.shape
    rows = cond.shape[0]
    tn = 1536
    return pl.pallas_call(
        _adaln_kernel,
        out_shape=jax.ShapeDtypeStruct((depth, rows, n), F32),
        grid=(depth, n // tn),
        in_specs=[
            pl.BlockSpec((rows, d), lambda l, j: (0, 0)),
            pl.BlockSpec((1, d, tn), lambda l, j: (l, 0, j)),
            pl.BlockSpec((1, 1, tn), lambda l, j: (l, 0, j)),
        ],
        out_specs=pl.BlockSpec((1, rows, tn), lambda l, j: (l, 0, j)),
        compiler_params=_params("parallel", "parallel"),
        name="adaln",
    )(cond, w_mod, b_mod.reshape(depth, 1, n))


def _mod_spec(mod):
    if mod.shape[0] == 1:
        return pl.BlockSpec((1,) + mod.shape[1:], lambda b, i: (0, 0, 0))
    return pl.BlockSpec((1,) + mod.shape[1:], lambda b, i: (b, 0, 0))


def _qkv_kernel(*refs, nq, nk, rope):
    x_ref, g_ref, mod_ref, w_ref, qg_ref, kg_ref, bd_ref = refs[:7]
    if rope:
        cos, sin = refs[7][...], refs[8][...]
        q_ref, k_ref, v_ref = refs[9:]
    else:
        cos = sin = None
        q_ref, k_ref, v_ref = refs[7:]
    h = _norm_mod(x_ref[0], g_ref[...], mod_ref[0, 0:1, :], mod_ref[0, 1:2, :]).astype(BF16)
    bd = bd_ref[...]

    def head(y, gain):
        y = _group_rms(y, gain, bd, HEAD_DIM)
        return _rope(y, cos, sin, HEAD_DIM // 4) if rope else y

    jobs = [(c, q_ref, c, lambda y: head(y, qg_ref[...])) for c in range(0, nq, MXU_COLS)]
    jobs += [(nq + c, k_ref, c, lambda y: head(y, kg_ref[...])) for c in range(0, nk, MXU_COLS)]
    jobs += [(nq + nk + c, v_ref, c, lambda y: y) for c in range(0, nk, MXU_COLS)]
    _pipelined_blocks(lambda col: _dot(h, w_ref[:, col:col + MXU_COLS]), jobs)


def _qkv_project(x, g, mod, w, q_gain, k_gain, nq, nk, kv_dtype, rope_tables=None):
    b, l, d = x.shape
    tl = min(TOKEN_TILE, l)
    rope = rope_tables is not None
    bd = _block_diag_ones(MXU_COLS, HEAD_DIM)
    args = [x, g.reshape(1, d), mod, w, _tile_gain(q_gain), _tile_gain(k_gain), bd]
    in_specs = [
        pl.BlockSpec((1, tl, d), lambda b_, i: (b_, i, 0)),
        _resident((1, d)),
        _mod_spec(mod),
        _resident(w.shape),
        _resident((1, MXU_COLS)),
        _resident((1, MXU_COLS)),
        _resident(bd.shape),
    ]
    if rope:
        args += list(rope_tables)
        in_specs += [pl.BlockSpec((tl, MXU_COLS), lambda b_, i: (i, 0))] * 2
    out = lambda n, dt: (jax.ShapeDtypeStruct((b, l, n), dt),
                         pl.BlockSpec((1, tl, n), lambda b_, i: (b_, i, 0)))
    shapes, specs = zip(out(nq, BF16), out(nk, kv_dtype), out(nk, kv_dtype))
    return pl.pallas_call(
        functools.partial(_qkv_kernel, nq=nq, nk=nk, rope=rope),
        out_shape=shapes,
        grid=(b, l // tl),
        in_specs=in_specs,
        out_specs=specs,
        compiler_params=_params("parallel", "parallel"),
        name="qkv_project",
    )(*args)


def _mla_project_kernel(*refs, rope):
    (x_ref, g_ref, mod_ref, wdq_ref, qlg_ref, wuq_ref, qng_ref, qpg_ref, bdn_ref, bdp_ref,
     wdkv_ref, kvg_ref, kpg_ref) = refs[:13]
    if rope:
        cos, sin = refs[13][...], refs[14][...]
        qn_ref, qp_ref, ckv_ref, kpe_ref = refs[15:]
    else:
        cos = sin = None
        qn_ref, qp_ref, ckv_ref, kpe_ref = refs[13:]
    h = _norm_mod(x_ref[0], g_ref[...], mod_ref[0, 0:1, :], mod_ref[0, 1:2, :]).astype(BF16)
    q_lora = wdq_ref.shape[1]
    ql = _row_rms(_dot(h, wdq_ref[...]), qlg_ref[...], q_lora).astype(BF16)
    n_nope = qn_ref.shape[2]
    n_pe = qp_ref.shape[2]
    def pe_head(y):
        y = _group_rms(y, qpg_ref[...], bdp_ref[...], MLA_ROPE)
        return _rope(y, cos, sin, MLA_ROPE // 4) if rope else y

    jobs = [(c, qn_ref, c, lambda y: _group_rms(y, qng_ref[...], bdn_ref[...], MLA_NOPE))
            for c in range(0, n_nope, MXU_COLS)]
    jobs += [(n_nope + c, qp_ref, c, pe_head) for c in range(0, n_pe, MXU_COLS)]
    _pipelined_blocks(lambda col: _dot(ql, wuq_ref[:, col:col + MXU_COLS]), jobs)
    ckv = _dot(h, wdkv_ref[:, :MLA_KV_LORA])
    ckv_ref[0] = _row_rms(ckv, kvg_ref[...], MLA_KV_LORA)
    kpe = _row_rms(_dot(h, wdkv_ref[:, MLA_KV_LORA:]), kpg_ref[...], MLA_ROPE)
    if rope:
        kpe = _rope(kpe, cos[:, :LANES], sin[:, :LANES], MLA_ROPE // 4)
    kpe_ref[0] = kpe


def _mla_project(x, g, mod, w_dq, q_lora_g, w_uq, q_nope_g, q_pe_g, w_dkv, kv_lora_g, k_pe_g,
                 rope_tables=None):
    b, l, d = x.shape
    tl = min(TOKEN_TILE, l)
    rope = rope_tables is not None
    n_nope, n_pe = N_HEADS * MLA_NOPE, N_HEADS * MLA_ROPE
    bdn = _block_diag_ones(MXU_COLS, MLA_NOPE)
    bdp = _block_diag_ones(MXU_COLS, MLA_ROPE)
    kpg = jnp.pad(k_pe_g, (0, LANES - MLA_ROPE)).reshape(1, LANES)
    args = [x, g.reshape(1, d), mod, w_dq, q_lora_g.reshape(1, -1), w_uq, _tile_gain(q_nope_g),
            _tile_gain(q_pe_g), bdn, bdp, w_dkv, kv_lora_g.reshape(1, -1), kpg]
    in_specs = [pl.BlockSpec((1, tl, d), lambda b_, i: (b_, i, 0)), _resident((1, d)), _mod_spec(mod)]
    in_specs += [_resident(a.shape) for a in args[3:]]
    if rope:
        args += list(rope_tables)
        in_specs += [pl.BlockSpec((tl, MXU_COLS), lambda b_, i: (i, 0))] * 2
    out = lambda n, dt: (jax.ShapeDtypeStruct((b, l, n), dt),
                         pl.BlockSpec((1, tl, n), lambda b_, i: (b_, i, 0)))
    shapes, specs = zip(out(n_nope, BF16), out(n_pe, BF16), out(MLA_KV_LORA, F32), out(LANES, F32))
    return pl.pallas_call(
        functools.partial(_mla_project_kernel, rope=rope),
        out_shape=shapes,
        grid=(b, l // tl),
        in_specs=in_specs,
        out_specs=specs,
        compiler_params=_params("parallel", "parallel"),
        name="mla_project",
    )(*args)


def _mla_expand_kernel(ckv_ref, kpe_ref, w_ref, kng_ref, bd_ref, tile_ref, kn_ref, v_ref, kpt_ref):
    c = ckv_ref[0].astype(BF16)
    n = kn_ref.shape[2]
    jobs = [(col, kn_ref, col, lambda y: _group_rms(y, kng_ref[...], bd_ref[...], MLA_NOPE))
            for col in range(0, n, MXU_COLS)]
    jobs += [(n + col, v_ref, col, lambda y: y) for col in range(0, n, MXU_COLS)]
    _pipelined_blocks(lambda col: _dot(c, w_ref[:, col:col + MXU_COLS]), jobs)
    kpt_ref[0] = _dot(kpe_ref[0].astype(BF16), tile_ref[...]).astype(kpt_ref.dtype)


def _mla_expand(ckv, kpe, w_ukv, k_nope_g):
    b, l, _ = ckv.shape
    tl = min(TOKEN_TILE, l)
    n = N_HEADS * MLA_NOPE
    pw = kpe.shape[2]
    bd = _block_diag_ones(MXU_COLS, MLA_NOPE)
    tile = np.zeros((pw, LANES), np.float32)
    for r in range(LANES // MLA_ROPE):
        tile[np.arange(MLA_ROPE), r * MLA_ROPE + np.arange(MLA_ROPE)] = 1.0
    tile = jnp.asarray(tile, BF16)
    out = lambda m: (jax.ShapeDtypeStruct((b, l, m), BF16), pl.BlockSpec((1, tl, m), lambda b_, i: (b_, i, 0)))
    shapes, specs = zip(out(n), out(n), out(LANES))
    return pl.pallas_call(
        _mla_expand_kernel,
        out_shape=shapes,
        grid=(b, l // tl),
        in_specs=[
            pl.BlockSpec((1, tl, MLA_KV_LORA), lambda b_, i: (b_, i, 0)),
            pl.BlockSpec((1, tl, pw), lambda b_, i: (b_, i, 0)),
            _resident(w_ukv.shape),
            _resident((1, MXU_COLS)),
            _resident(bd.shape),
            _resident(tile.shape),
        ],
        out_specs=specs,
        compiler_params=_params("parallel", "parallel"),
        name="mla_expand",
    )(ckv, kpe, w_ukv, _tile_gain(k_nope_g), bd, tile)


def _attn_kernel(*refs, mode, ctx, mla, sink, tq, rows, pp, group, bounded):
    it = iter(refs)
    q_ref, k_ref, v_ref = next(it), next(it), next(it)
    kc_ref, vc_ref = (next(it), next(it)) if ctx else (None, None)
    q2_ref, k2_ref = (next(it), next(it)) if mla else (None, None)
    kc2_ref = next(it) if (mla and ctx) else None
    sink_ref = next(it) if sink else None
    tbl_ref = next(it) if mode == "na" else None
    bound_ref = next(it) if bounded else None
    o_ref = next(it)

    g = pl.program_id(1)
    t = pl.program_id(2)
    lane = lax.broadcasted_iota(jnp.int32, (1, LANES), 1)
    plans = [_attn_pair_plan(i, g, t, lane, q_ref, k_ref, v_ref, kc_ref, vc_ref, q2_ref, k2_ref, kc2_ref,
                             sink_ref, tbl_ref, mode=mode, ctx=ctx, mla=mla, sink=sink, tq=tq, rows=rows,
                             pp=pp, group=group)
             for i in range(pp)]
    floor = bound_ref[0] if bounded else None

    def scores(i, j):
        kk, _, fix, _ = plans[i][1][j]
        s = _dot(kk, plans[i][0])
        return s if fix is None else fix(s)

    n_chunks = len(plans[0][1])
    state = [None] * pp
    s_next = [scores(i, 0) for i in range(pp)]
    for j in range(n_chunks):
        for i in range(pp):
            s = s_next[i]
            if j + 1 < n_chunks:
                s_next[i] = scores(i, j + 1)
            _, vv, _, exact_max = plans[i][1][j]
            state[i] = _softmax_step(state[i], s, vv, plans[i][2], floor, exact_max or not bounded)
    for i in range(pp):
        m, den, acc = state[i]
        if sink:
            den = den + jnp.exp2(plans[i][2] - m)
        o = acc / den
        o_pair = jnp.concatenate([o[:HEAD_DIM, :tq], o[HEAD_DIM:, tq:]], axis=0)
        o_ref[0, :, i * LANES:(i + 1) * LANES] = o_pair.T.astype(o_ref.dtype)


def _softmax_step(state, s, vv, sk, floor, exact_max):
    def probs(m):
        pr = jnp.exp2(s - m)
        return pr.astype(BF16), jnp.sum(pr, axis=0, keepdims=True)

    if state is None:
        m = None
        if floor is not None:
            m = jnp.full((1, s.shape[1]), floor, F32)
        if sk is not None:
            m = sk if m is None else jnp.maximum(m, sk)
        if exact_max:
            m_c = jnp.max(s, axis=0, keepdims=True)
            m = m_c if m is None else jnp.maximum(m, m_c)
        pr, total = probs(m)
        return m, total, _dot_tn(vv, pr)
    m, den, acc = state
    if exact_max:
        m_new = jnp.maximum(m, jnp.max(s, axis=0, keepdims=True))
        alpha = jnp.exp2(m - m_new)
        pr, total = probs(m_new)
        return m_new, alpha * den + total, alpha * acc + _dot_tn(vv, pr)
    pr, total = probs(m)
    return m, den + total, acc + _dot_tn(vv, pr)


def _attn_pair_plan(i, g, t, lane, q_ref, k_ref, v_ref, kc_ref, vc_ref, q2_ref, k2_ref, kc2_ref, sink_ref,
                    tbl_ref, *, mode, ctx, mla, sink, tq, rows, pp, group):
    ql = slice(i * LANES, (i + 1) * LANES)
    kvi = i // group if pp >= group else 0
    kl = slice(kvi * LANES, (kvi + 1) * LANES)

    row = lax.broadcasted_iota(jnp.int32, (LANES, 1), 0)

    def stack(x, m0, m1):
        xt = x.astype(F32).T
        return jnp.concatenate([jnp.where(m0, xt, 0.0), jnp.where(m1, xt, 0.0)], axis=1).astype(BF16)

    qs = stack(q_ref[0, :, ql], row < HEAD_DIM, row >= HEAD_DIM)
    if mla:
        if pp == 1:
            q2, base = q2_ref[0], (g % 2) * (2 * MLA_ROPE)
        else:
            q2, base = q2_ref[0, :, (i // 2) * LANES:(i // 2 + 1) * LANES], (i % 2) * (2 * MLA_ROPE)
        m0 = (row >= base) & (row < base + MLA_ROPE)
        m1 = (row >= base + MLA_ROPE) & (row < base + 2 * MLA_ROPE)
        qs = jnp.concatenate([qs, stack(q2, m0, m1)], axis=0)

    def keys(ref, ref2, rows_):
        kk = ref[0, rows_, kl].astype(BF16)
        if mla:
            kk = jnp.concatenate([kk, ref2[0, rows_, :].astype(BF16)], axis=1)
        return kk

    chunks = []
    if ctx:
        lc = kc_ref.shape[1]
        kc_len = min(KEY_CHUNK, lc)
        for c in range(0, lc, kc_len):
            rows_ = slice(c, c + kc_len)
            chunks.append((keys(kc_ref, kc2_ref, rows_), vc_ref[0, rows_, kl].astype(BF16), None, True))
    if mode == "full":
        lk = k_ref.shape[1]
        kc_len = min(KEY_CHUNK, lk)
        for c in range(0, lk, kc_len):
            rows_ = slice(c, c + kc_len)
            chunks.append((keys(k_ref, k2_ref, rows_), v_ref[0, rows_, kl].astype(BF16), None, False))
    elif mode == "swa":
        win = tq + 2 * SWA_WINDOW
        ws = pl.multiple_of(jnp.clip(t * tq - SWA_WINDOW, 0, k_ref.shape[1] - win), SWA_WINDOW)

        def band(s):
            kpos = ws + lax.broadcasted_iota(jnp.int32, (win, 1), 0)
            qi = t * tq + lax.broadcasted_iota(jnp.int32, (1, tq), 1)
            qpos = jnp.concatenate([qi, qi], axis=1)
            return jnp.where(jnp.abs(kpos - qpos) <= SWA_WINDOW, s, NEG_INF)

        chunks.append((k_ref[0, pl.ds(ws, win), kl], v_ref[0, pl.ds(ws, win), kl], band, False))
    else:
        qrows = tq // GRID_W
        wrows = qrows + NA_WIN_ROWS
        r0 = t * qrows
        ws_row = jnp.clip(r0 - NA_WIN_ROWS // 2, 0, rows - wrows)
        ws = pl.multiple_of(ws_row * GRID_W, GRID_W)

        def window_bias(s):
            e0 = ws_row - r0 + (NA_WIN_ROWS - 1) + qrows - 1
            out_rows = []
            for j in range(wrows):
                kr = ws_row + j
                blocks = []
                for hh in range(2):
                    for ii in range(qrows // 2):
                        ok = []
                        for qr in (r0 + 2 * ii, r0 + 2 * ii + 1):
                            rs = jnp.clip(qr - NA_WIN_ROWS // 2, 0, rows - NA_WIN_ROWS)
                            ok.append(((kr >= rs) & (kr < rs + NA_WIN_ROWS)).astype(jnp.int32))
                        valid = jnp.where(lane < GRID_W, ok[0], ok[1]) > 0
                        lo = hh * tq + ii * LANES
                        blk = (s[j * GRID_W:(j + 1) * GRID_W, lo:lo + LANES]
                               + tbl_ref[2 * i + hh, e0 + j - 2 * ii])
                        blocks.append(jnp.where(valid, blk, NEG_INF))
                out_rows.append(jnp.concatenate(blocks, axis=1))
            return jnp.concatenate(out_rows, axis=0)

        chunks.append((k_ref[0, pl.ds(ws, wrows * GRID_W), kl], v_ref[0, pl.ds(ws, wrows * GRID_W), kl],
                       window_bias, False))

    sk = None
    if sink:
        first = lax.broadcasted_iota(jnp.int32, (1, 2 * tq), 1) < tq
        head = 2 * (g * pp + i)
        sk = jnp.where(first, sink_ref[head], sink_ref[head + 1])
    return qs, chunks, sk


def _attention(q, k, v, *, mode="full", kc=None, vc=None, q2=None, k2=None, kc2=None, sink=None,
               na_table=None, bound=None, pp=1, q_tile=Q_TILE):
    b, lq, nq = q.shape
    lk = k.shape[1]
    pairs = nq // LANES
    group = pairs // (k.shape[2] // LANES)
    tq = min(q_tile, lq)
    ctx, mla = kc is not None, q2 is not None
    assert pairs % pp == 0 and (pp % group == 0 or group % pp == 0)
    assert not mla or pp == 1 or pp % 2 == 0

    kv_lanes = LANES * max(1, pp // group)
    qmap = lambda b_, g, t: (b_, t, g)
    if pp >= group:
        kvmap = lambda b_, g, t: (b_, 0, g)
    else:
        kvmap = lambda b_, g, t: (b_, 0, (g * pp) // group)
    shared = lambda b_, g, t: (b_, 0, 0)
    args = [q, k, v]
    in_specs = [pl.BlockSpec((1, tq, pp * LANES), qmap),
                pl.BlockSpec((1, lk, kv_lanes), kvmap),
                pl.BlockSpec((1, lk, kv_lanes), kvmap)]
    if ctx:
        lc = kc.shape[1]
        args += [kc, vc]
        in_specs += [pl.BlockSpec((1, lc, kv_lanes), kvmap)] * 2
    if mla:
        args += [q2, k2]
        if pp == 1:
            in_specs += [pl.BlockSpec((1, tq, LANES), lambda b_, g, t: (b_, t, g // 2))]
        else:
            in_specs += [pl.BlockSpec((1, tq, pp * LANES // 2), qmap)]
        in_specs += [pl.BlockSpec((1, lk, LANES), shared)]
        if ctx:
            args += [kc2]
            in_specs += [pl.BlockSpec((1, kc2.shape[1], LANES), shared)]
    if sink is not None:
        args += [sink]
        in_specs += [pl.BlockSpec(memory_space=pltpu.SMEM)]
    if mode == "na":
        args += [na_table]
        in_specs += [pl.BlockSpec((2 * pp,) + na_table.shape[1:], lambda b_, g, t: (g, 0, 0, 0))]
    if bound is not None:
        args += [bound]
        in_specs += [pl.BlockSpec(memory_space=pltpu.SMEM)]
    return pl.pallas_call(
        functools.partial(_attn_kernel, mode=mode, ctx=ctx, mla=mla, sink=sink is not None, tq=tq,
                          rows=lq // GRID_W, pp=pp, group=group, bounded=bound is not None),
        out_shape=jax.ShapeDtypeStruct((b, lq, nq), BF16),
        grid=(b, pairs // pp, lq // tq),
        in_specs=in_specs,
        out_specs=pl.BlockSpec((1, tq, pp * LANES), qmap),
        compiler_params=_params("parallel", "parallel", "parallel"),
        name="attention_" + mode,
    )(*args)


def _attention_guarded(score_bound, *args, **kwargs):
    bound = jnp.reshape(score_bound * BOUND_SLACK, (1,)).astype(F32)
    return lax.cond(bound[0] <= MAX_SCORE_BOUND,
                    lambda: _attention(*args, bound=bound, **kwargs),
                    lambda: _attention(*args, **kwargs))


def _out_ffn_kernel(x_ref, a_ref, mod_ref, g_ref, wo_ref, wg_ref, wu_ref, wd_ref, o_ref, *, chunks):
    mod = lambda r: mod_ref[0, r:r + 1, :]
    x1 = x_ref[0] + mod(2) * _dot(a_ref[0], wo_ref[...])
    h = _norm_mod(x1, g_ref[...], mod(3), mod(4)).astype(BF16)
    hidden = wg_ref.shape[1]
    step = hidden // chunks
    gate_up = lambda c: (_dot(h, wg_ref[:, c:c + step]), _dot(h, wu_ref[:, c:c + step]))
    y = None
    nxt = gate_up(0)
    for c in range(0, hidden, step):
        gate, up = nxt
        if c + step < hidden:
            nxt = gate_up(c + step)
        act = (gate * jax.nn.sigmoid(gate) * up).astype(BF16)
        part = _dot(act, wd_ref[c:c + step, :])
        y = part if y is None else y + part
    o_ref[0] = x1 + mod(5) * y


def _out_ffn(x, attn, mod, g2, w_o, w_gate, w_up, w_down):
    b, l, d = x.shape
    tl = min(TOKEN_TILE, l)
    tok = lambda n: pl.BlockSpec((1, tl, n), lambda b_, i: (b_, i, 0))
    return pl.pallas_call(
        functools.partial(_out_ffn_kernel, chunks=11),
        out_shape=jax.ShapeDtypeStruct((b, l, d), F32),
        grid=(b, l // tl),
        in_specs=[tok(d), tok(attn.shape[2]), _mod_spec(mod), _resident((1, d)),
                  _resident(w_o.shape), _resident(w_gate.shape), _resident(w_up.shape),
                  _resident(w_down.shape)],
        out_specs=tok(d),
        compiler_params=_params("parallel", "parallel"),
        name="out_ffn",
    )(x, attn, mod, g2.reshape(1, d), w_o, w_gate, w_up, w_down)


def _max_abs(a):
    return jnp.max(jnp.abs(a))


def _block_diag_ones(n, group):
    idx = np.arange(n) // group
    return jnp.asarray(idx[:, None] == idx[None, :], BF16)


def _tile_gain(g, scale=1.0):
    return jnp.tile(g * scale, MXU_COLS // g.shape[0]).reshape(1, MXU_COLS)


def _rope_tables(n_tokens, r):
    half, quarter = r // 2, r // 4
    t = jnp.arange(n_tokens, dtype=jnp.int32)
    rows, cols = (t // GRID_W).astype(F32), (t % GRID_W).astype(F32)
    inv_freq = ROPE_THETA ** (-jnp.arange(quarter, dtype=F32) / quarter)
    d = np.arange(MXU_COLS) % r
    use_row = jnp.asarray(d < half)
    freq = inv_freq[d % quarter]
    ang = jnp.where(use_row[None, :], rows[:, None], cols[:, None]) * freq[None, :]
    sign = jnp.asarray(np.where((d % half) < quarter, -1.0, 1.0), F32)
    return jnp.cos(ang), jnp.sin(ang) * sign[None, :]


_GQA_HEADS = np.array([8 * (p // 4) + 4 * e + (p % 4) for p in range(N_HEADS // 2) for e in range(2)])
_GQA_COLS = (_GQA_HEADS[:, None] * HEAD_DIM + np.arange(HEAD_DIM)[None, :]).reshape(-1)


def _na_bias_table(rpb, q_rows):
    qc = np.arange(GRID_W)[None, :]
    kc = np.arange(GRID_W)[:, None]
    start = np.clip(qc - NA_WIN_COLS // 2, 0, GRID_W - NA_WIN_COLS)
    valid = (kc >= start) & (kc < start + NA_WIN_COLS)
    dcol = np.clip(kc - qc + NA_WIN_COLS - 1, 0, 2 * NA_WIN_COLS - 2)
    t = jnp.where(jnp.asarray(valid)[None, None], rpb[:, :, dcol], NEG_INF)
    pad = q_rows
    t = jnp.pad(t, ((0, 0), (pad, pad), (0, 0), (0, 0)))
    n_e = 2 * q_rows + 2 * NA_WIN_ROWS - 2
    first = t[:, 1:1 + n_e]
    second = t[:, 0:n_e]
    return jnp.concatenate([first, second], axis=-1)


def kernel(x_prompt, x_sample, cache_na_k, cache_na_v, cache_swa_k, cache_swa_v, cache_mla_ckv, cache_mla_kpe, cache_gqa_k, cache_gqa_v, c, c_ctx, norm1_g, norm2_g, w_mod, b_mod, na_w_qkv, na_q_g, na_k_g, na_rpb, na_w_o, swa_w_qkv, swa_q_g, swa_k_g, swa_sink, swa_w_o, mla_w_dq, mla_q_lora_g, mla_w_uq, mla_q_nope_g, mla_q_pe_g, mla_w_dkv, mla_kv_lora_g, mla_k_pe_g, mla_w_ukv, mla_k_nope_g, mla_w_o, gqa_w_qkv, gqa_q_g, gqa_k_g, gqa_w_o, ffn_w_gate, ffn_w_up, ffn_w_down):
    depth, d = norm1_g.shape
    bp, lp, _ = x_prompt.shape
    bs, ls, _ = x_sample.shape
    hq, hkv = N_HEADS * HEAD_DIM, N_KV_HEADS * HEAD_DIM
    scale = HEAD_DIM ** -0.5 * LOG2E

    cond = jnp.concatenate([c_ctx[None, :], c, jnp.zeros((16 - 1 - bs, d), F32)], axis=0)
    mods = _adaln(cond, w_mod, b_mod).reshape(depth, 16, 6, d)
    rope64 = _rope_tables(ls, HEAD_DIM)
    rope32 = _rope_tables(ls, MLA_ROPE)
    flat = lambda a, r: a[:, r].reshape(a.shape[0], a.shape[2], -1)

    xp, xs = x_prompt, x_sample
    outs = {}
    for li in range(depth):
        kind, r = li % 4, li // 4
        mp, ms = mods[li, 0:1], mods[li, 1:1 + bs]
        g1 = norm1_g[li]
        if kind in (0, 1, 3):
            w_qkv, q_g, k_g, w_o = ((na_w_qkv, na_q_g, na_k_g, na_w_o), (swa_w_qkv, swa_q_g, swa_k_g, swa_w_o),
                                    None, (gqa_w_qkv, gqa_q_g, gqa_k_g, gqa_w_o))[kind]
            w_qkv, w_o = w_qkv[r], w_o[r]
            nk = hq if kind == 0 else hkv
            if kind != 0:
                w_qkv = jnp.concatenate([w_qkv[:, :hq][:, _GQA_COLS], w_qkv[:, hq:]], axis=1)
                w_o = w_o[_GQA_COLS, :]
            w_qkv, w_o = w_qkv.astype(BF16), w_o.astype(BF16)
            qp, kp, vp = _qkv_project(xp, g1, mp, w_qkv, q_g[r] * scale, k_g[r], hq, nk, F32)
            qs, ks, vs = _qkv_project(xs, g1, ms, w_qkv, q_g[r] * scale, k_g[r], hq, nk, BF16,
                                      rope_tables=None if kind == 0 else rope64)
            qk_bound = HEAD_DIM * _max_abs(q_g[r] * scale) * _max_abs(k_g[r])
            if kind == 0:
                kc, vc = flat(cache_na_k, r), flat(cache_na_v, r)
                rpb = na_rpb[r] * LOG2E
                ap = _attention_guarded(qk_bound, qp, kp, vp, pp=PROMPT_PAIRS)
                as_ = _attention_guarded(qk_bound + _max_abs(rpb), qs, ks, vs, mode="na", kc=kc, vc=vc,
                                         na_table=_na_bias_table(rpb, WINDOW_Q_TILE // GRID_W),
                                         pp=WINDOW_PAIRS, q_tile=WINDOW_Q_TILE)
                outs["na_k"], outs["na_v"] = kp, vp
            elif kind == 1:
                kc, vc = flat(cache_swa_k, r), flat(cache_swa_v, r)
                sink = swa_sink[r][_GQA_HEADS] * LOG2E
                ap = _attention_guarded(qk_bound, qp, kp, vp, sink=sink, pp=PROMPT_PAIRS)
                as_ = _attention_guarded(qk_bound, qs, ks, vs, mode="swa", kc=kc, vc=vc, sink=sink,
                                         pp=WINDOW_PAIRS, q_tile=WINDOW_Q_TILE)
                outs["swa_k"], outs["swa_v"] = kp, vp
            else:
                kc, vc = flat(cache_gqa_k, r), flat(cache_gqa_v, r)
                ap = _attention_guarded(qk_bound, qp, kp, vp, pp=PROMPT_PAIRS)
                as_ = _attention_guarded(qk_bound, qs, ks, vs, kc=kc, vc=vc, pp=LATENT_PAIRS)
                outs["gqa_k"], outs["gqa_v"] = kp, vp
        else:
            dk = MLA_NOPE + MLA_ROPE
            w_uq = mla_w_uq[r].reshape(-1, N_HEADS, dk)
            w_uq = jnp.concatenate([w_uq[:, :, :MLA_NOPE].reshape(-1, N_HEADS * MLA_NOPE),
                                    w_uq[:, :, MLA_NOPE:].reshape(-1, N_HEADS * MLA_ROPE)], axis=1).astype(BF16)
            w_ukv = mla_w_ukv[r].reshape(-1, N_HEADS, MLA_NOPE + MLA_V)
            w_ukv = jnp.concatenate([w_ukv[:, :, :MLA_NOPE].reshape(-1, N_HEADS * MLA_NOPE),
                                     w_ukv[:, :, MLA_NOPE:].reshape(-1, N_HEADS * MLA_V)], axis=1).astype(BF16)
            w_dkv = jnp.pad(mla_w_dkv[r], ((0, 0), (0, 2 * LANES - MLA_KV_LORA - MLA_ROPE))).astype(BF16)
            w_dq, w_o = mla_w_dq[r].astype(BF16), mla_w_o[r].astype(BF16)
            qscale = dk ** -0.5 * LOG2E
            proj = functools.partial(
                _mla_project, w_dq=w_dq, q_lora_g=mla_q_lora_g[r], w_uq=w_uq, q_nope_g=mla_q_nope_g[r] * qscale,
                q_pe_g=mla_q_pe_g[r] * qscale, w_dkv=w_dkv, kv_lora_g=mla_kv_lora_g[r], k_pe_g=mla_k_pe_g[r])
            qn_p, qpe_p, ckv_p, kpe_p = proj(xp, g1, mp)
            qn_s, qpe_s, ckv_s, kpe_s = proj(xs, g1, ms, rope_tables=rope32)
            kn_p, v_p, kpt_p = _mla_expand(ckv_p, kpe_p, w_ukv, mla_k_nope_g[r])
            kn_s, v_s, kpt_s = _mla_expand(ckv_s, kpe_s, w_ukv, mla_k_nope_g[r])
            kn_c, v_c, kpt_c = _mla_expand(cache_mla_ckv[:, r], cache_mla_kpe[:, r], w_ukv, mla_k_nope_g[r])
            q_norm = jnp.sqrt(MLA_NOPE * _max_abs(mla_q_nope_g[r] * qscale) ** 2
                              + MLA_ROPE * _max_abs(mla_q_pe_g[r] * qscale) ** 2)
            k_norm = jnp.sqrt(MLA_NOPE * _max_abs(mla_k_nope_g[r]) ** 2 + MLA_ROPE * _max_abs(mla_k_pe_g[r]) ** 2)
            ap = _attention_guarded(q_norm * k_norm, qn_p, kn_p, v_p, q2=qpe_p, k2=kpt_p, pp=PROMPT_PAIRS)
            as_ = _attention_guarded(q_norm * k_norm, qn_s, kn_s, v_s, kc=kn_c, vc=v_c, q2=qpe_s, k2=kpt_s,
                                     kc2=kpt_c, pp=LATENT_PAIRS)
            outs["mla_ckv"], outs["mla_kpe"] = ckv_p, kpe_p[:, :, :MLA_ROPE]
        wg, wu, wd = ffn_w_gate[li].astype(BF16), ffn_w_up[li].astype(BF16), ffn_w_down[li].astype(BF16)
        xp = _out_ffn(xp, ap, mp, norm2_g[li], w_o, wg, wu, wd)
        xs = _out_ffn(xs, as_, ms, norm2_g[li], w_o, wg, wu, wd)

    heads = lambda a, h: a.reshape(bp, 1, lp, h, HEAD_DIM)
    return (xp, xs, heads(outs["na_k"], N_HEADS), heads(outs["na_v"], N_HEADS),
            heads(outs["swa_k"], N_KV_HEADS), heads(outs["swa_v"], N_KV_HEADS),
            outs["mla_ckv"][:, None], outs["mla_kpe"][:, None],
            heads(outs["gqa_k"], N_KV_HEADS), heads(outs["gqa_v"], N_KV_HEADS))
```

```python
import functools

import numpy as np
import jax
import jax.numpy as jnp
from jax import lax
from jax.experimental import pallas as pl
from jax.experimental.pallas import tpu as pltpu

GRID_W = 64
HEAD_DIM = 64
N_HEADS = 16
N_KV_HEADS = 4
NA_WIN_ROWS = 8
NA_WIN_COLS = 16
SWA_WINDOW = 128
MLA_KV_LORA = 128
MLA_NOPE = 64
MLA_ROPE = 32
MLA_V = 64
ROPE_THETA = 10000.0
EPS = 1e-6
NEG_INF = -1e30

LANES = 128
MXU_COLS = 256
VMEM_LIMIT = 56 * 1024 * 1024
TOKEN_TILE = 512
Q_TILE = 256
KEY_CHUNK = 256
LOG2E = 1.4426950408889634
PROMPT_PAIRS = 8
LATENT_PAIRS = 4
WINDOW_Q_TILE = 128
WINDOW_PAIRS = 8
BOUND_SLACK = 1.01
MAX_SCORE_BOUND = 48.0

F32 = jnp.float32
BF16 = jnp.bfloat16


def _params(*sem):
    return pltpu.CompilerParams(dimension_semantics=sem, vmem_limit_bytes=VMEM_LIMIT)


def _resident(shape):
    nd = len(shape)
    return pl.BlockSpec(shape, lambda *_: (0,) * nd, pipeline_mode=pl.Buffered(1))


def _dot(a, b):
    return jnp.dot(a, b, preferred_element_type=F32)


def _dot_nt(a, b):
    return lax.dot_general(a, b, (((1,), (1,)), ((), ())), preferred_element_type=F32)


def _dot_tn(a, b):
    return lax.dot_general(a, b, (((0,), (0,)), ((), ())), preferred_element_type=F32)


def _norm_mod(x, g, shift, scale):
    ms = jnp.mean(x * x, axis=-1, keepdims=True)
    y = x * lax.rsqrt(ms + EPS) * g
    return y * (1.0 + scale) + shift


def _row_rms(x, g, n):
    ms = jnp.sum(x * x, axis=-1, keepdims=True) * (1.0 / n)
    return x * lax.rsqrt(ms + EPS) * g


def _group_rms(y, gain, bd, group):
    ss = _dot((y * y).astype(BF16), bd)
    return y * lax.rsqrt(ss * (1.0 / group) + EPS) * gain


def _pipelined_blocks(project, jobs):
    y_next = project(jobs[0][0])
    for j, (_, ref, col, finish) in enumerate(jobs):
        y = y_next
        if j + 1 < len(jobs):
            y_next = project(jobs[j + 1][0])
        ref[0, :, col:col + MXU_COLS] = finish(y).astype(ref.dtype)


def _rope(y, cos, sin_signed, quarter):
    n = y.shape[-1]
    lane = lax.broadcasted_iota(jnp.int32, (1, n), 1)
    first = (lane & quarter) == 0
    partner = jnp.where(first, pltpu.roll(y, n - quarter, 1), pltpu.roll(y, quarter, 1))
    return y * cos + partner * sin_signed


def _adaln_kernel(cond_ref, w_ref, b_ref, o_ref):
    c = cond_ref[...]
    a = (c * jax.nn.sigmoid(c)).astype(BF16)
    o_ref[0] = _dot(a, w_ref[0].astype(BF16)) + b_ref[0]


def _adaln(cond, w_mod, b_mod):
    depth, d, n = w_mod.shape
    rows = cond.shape[0]
    tn = 1536
    return pl.pallas_call(
        _adaln_kernel,
        out_shape=jax.ShapeDtypeStruct((depth, rows, n), F32),
        grid=(depth, n // tn),
        in_specs=[
            pl.BlockSpec((rows, d), lambda l, j: (0, 0)),
            pl.BlockSpec((1, d, tn), lambda l, j: (l, 0, j)),
            pl.BlockSpec((1, 1, tn), lambda l, j: (l, 0, j)),
        ],
        out_specs=pl.BlockSpec((1, rows, tn), lambda l, j: (l, 0, j)),
        compiler_params=_params("parallel", "parallel"),
        name="adaln",
    )(cond, w_mod, b_mod.reshape(depth, 1, n))


def _mod_spec(mod):
    if mod.shape[0] == 1:
        return pl.BlockSpec((1,) + mod.shape[1:], lambda b, i: (0, 0, 0))
    return pl.BlockSpec((1,) + mod.shape[1:], lambda b, i: (b, 0, 0))


def _qkv_kernel(*refs, nq, nk, rope):
    x_ref, g_ref, mod_ref, w_ref, qg_ref, kg_ref, bd_ref = refs[:7]
    if rope:
        cos, sin = refs[7][...], refs[8][...]
        q_ref, k_ref, v_ref = refs[9:]
    else:
        cos = sin = None
        q_ref, k_ref, v_ref = refs[7:]
    h = _norm_mod(x_ref[0], g_ref[...], mod_ref[0, 0:1, :], mod_ref[0, 1:2, :]).astype(BF16)
    bd = bd_ref[...]

    def head(y, gain):
        y = _group_rms(y, gain, bd, HEAD_DIM)
        return _rope(y, cos, sin, HEAD_DIM // 4) if rope else y

    jobs = [(c, q_ref, c, lambda y: head(y, qg_ref[...])) for c in range(0, nq, MXU_COLS)]
    jobs += [(nq + c, k_ref, c, lambda y: head(y, kg_ref[...])) for c in range(0, nk, MXU_COLS)]
    jobs += [(nq + nk + c, v_ref, c, lambda y: y) for c in range(0, nk, MXU_COLS)]
    _pipelined_blocks(lambda col: _dot(h, w_ref[:, col:col + MXU_COLS]), jobs)


def _qkv_project(x, g, mod, w, q_gain, k_gain, nq, nk, kv_dtype, rope_tables=None):
    b, l, d = x.shape
    tl = min(TOKEN_TILE, l)
    rope = rope_tables is not None
    bd = _block_diag_ones(MXU_COLS, HEAD_DIM)
    args = [x, g.reshape(1, d), mod, w, _tile_gain(q_gain), _tile_gain(k_gain), bd]
    in_specs = [
        pl.BlockSpec((1, tl, d), lambda b_, i: (b_, i, 0)),
        _resident((1, d)),
        _mod_spec(mod),
        _resident(w.shape),
        _resident((1, MXU_COLS)),
        _resident((1, MXU_COLS)),
        _resident(bd.shape),
    ]
    if rope:
        args += list(rope_tables)
        in_specs += [pl.BlockSpec((tl, MXU_COLS), lambda b_, i: (i, 0))] * 2
    out = lambda n, dt: (jax.ShapeDtypeStruct((b, l, n), dt),
                         pl.BlockSpec((1, tl, n), lambda b_, i: (b_, i, 0)))
    shapes, specs = zip(out(nq, BF16), out(nk, kv_dtype), out(nk, kv_dtype))
    return pl.pallas_call(
        functools.partial(_qkv_kernel, nq=nq, nk=nk, rope=rope),
        out_shape=shapes,
        grid=(b, l // tl),
        in_specs=in_specs,
        out_specs=specs,
        compiler_params=_params("parallel", "parallel"),
        name="qkv_project",
    )(*args)


def _mla_project_kernel(*refs, rope):
    (x_ref, g_ref, mod_ref, wdq_ref, qlg_ref, wuq_ref, qng_ref, qpg_ref, bdn_ref, bdp_ref,
     wdkv_ref, kvg_ref, kpg_ref) = refs[:13]
    if rope:
        cos, sin = refs[13][...], refs[14][...]
        qn_ref, qp_ref, ckv_ref, kpe_ref = refs[15:]
    else:
        cos = sin = None
        qn_ref, qp_ref, ckv_ref, kpe_ref = refs[13:]
    h = _norm_mod(x_ref[0], g_ref[...], mod_ref[0, 0:1, :], mod_ref[0, 1:2, :]).astype(BF16)
    q_lora = wdq_ref.shape[1]
    ql = _row_rms(_dot(h, wdq_ref[...]), qlg_ref[...], q_lora).astype(BF16)
    n_nope = qn_ref.shape[2]
    n_pe = qp_ref.shape[2]
    def pe_head(y):
        y = _group_rms(y, qpg_ref[...], bdp_ref[...], MLA_ROPE)
        return _rope(y, cos, sin, MLA_ROPE // 4) if rope else y

    jobs = [(c, qn_ref, c, lambda y: _group_rms(y, qng_ref[...], bdn_ref[...], MLA_NOPE))
            for c in range(0, n_nope, MXU_COLS)]
    jobs += [(n_nope + c, qp_ref, c, pe_head) for c in range(0, n_pe, MXU_COLS)]
    _pipelined_blocks(lambda col: _dot(ql, wuq_ref[:, col:col + MXU_COLS]), jobs)
    ckv = _dot(h, wdkv_ref[:, :MLA_KV_LORA])
    ckv_ref[0] = _row_rms(ckv, kvg_ref[...], MLA_KV_LORA)
    kpe = _row_rms(_dot(h, wdkv_ref[:, MLA_KV_LORA:]), kpg_ref[...], MLA_ROPE)
    if rope:
        kpe = _rope(kpe, cos[:, :LANES], sin[:, :LANES], MLA_ROPE // 4)
    kpe_ref[0] = kpe


def _mla_project(x, g, mod, w_dq, q_lora_g, w_uq, q_nope_g, q_pe_g, w_dkv, kv_lora_g, k_pe_g,
                 rope_tables=None):
    b, l, d = x.shape
    tl = min(TOKEN_TILE, l)
    rope = rope_tables is not None
    n_nope, n_pe = N_HEADS * MLA_NOPE, N_HEADS * MLA_ROPE
    bdn = _block_diag_ones(MXU_COLS, MLA_NOPE)
    bdp = _block_diag_ones(MXU_COLS, MLA_ROPE)
    kpg = jnp.pad(k_pe_g, (0, LANES - MLA_ROPE)).reshape(1, LANES)
    args = [x, g.reshape(1, d), mod, w_dq, q_lora_g.reshape(1, -1), w_uq, _tile_gain(q_nope_g),
            _tile_gain(q_pe_g), bdn, bdp, w_dkv, kv_lora_g.reshape(1, -1), kpg]
    in_specs = [pl.BlockSpec((1, tl, d), lambda b_, i: (b_, i, 0)), _resident((1, d)), _mod_spec(mod)]
    in_specs += [_resident(a.shape) for a in args[3:]]
    if rope:
        args += list(rope_tables)
        in_specs += [pl.BlockSpec((tl, MXU_COLS), lambda b_, i: (i, 0))] * 2
    out = lambda n, dt: (jax.ShapeDtypeStruct((b, l, n), dt),
                         pl.BlockSpec((1, tl, n), lambda b_, i: (b_, i, 0)))
    shapes, specs = zip(out(n_nope, BF16), out(n_pe, BF16), out(MLA_KV_LORA, F32), out(LANES, F32))
    return pl.pallas_call(
        functools.partial(_mla_project_kernel, rope=rope),
        out_shape=shapes,
        grid=(b, l // tl),
        in_specs=in_specs,
        out_specs=specs,
        compiler_params=_params("parallel", "parallel"),
        name="mla_project",
    )(*args)


def _mla_expand_kernel(ckv_ref, kpe_ref, w_ref, kng_ref, bd_ref, tile_ref, kn_ref, v_ref, kpt_ref):
    c = ckv_ref[0].astype(BF16)
    n = kn_ref.shape[2]
    jobs = [(col, kn_ref, col, lambda y: _group_rms(y, kng_ref[...], bd_ref[...], MLA_NOPE))
            for col in range(0, n, MXU_COLS)]
    jobs += [(n + col, v_ref, col, lambda y: y) for col in range(0, n, MXU_COLS)]
    _pipelined_blocks(lambda col: _dot(c, w_ref[:, col:col + MXU_COLS]), jobs)
    kpt_ref[0] = _dot(kpe_ref[0].astype(BF16), tile_ref[...]).astype(kpt_ref.dtype)


def _mla_expand(ckv, kpe, w_ukv, k_nope_g):
    b, l, _ = ckv.shape
    tl = min(TOKEN_TILE, l)
    n = N_HEADS * MLA_NOPE
    pw = kpe.shape[2]
    bd = _block_diag_ones(MXU_COLS, MLA_NOPE)
    tile = np.zeros((pw, LANES), np.float32)
    for r in range(LANES // MLA_ROPE):
        tile[np.arange(MLA_ROPE), r * MLA_ROPE + np.arange(MLA_ROPE)] = 1.0
    tile = jnp.asarray(tile, BF16)
    out = lambda m: (jax.ShapeDtypeStruct((b, l, m), BF16), pl.BlockSpec((1, tl, m), lambda b_, i: (b_, i, 0)))
    shapes, specs = zip(out(n), out(n), out(LANES))
    return pl.pallas_call(
        _mla_expand_kernel,
        out_shape=shapes,
        grid=(b, l // tl),
        in_specs=[
            pl.BlockSpec((1, tl, MLA_KV_LORA), lambda b_, i: (b_, i, 0)),
            pl.BlockSpec((1, tl, pw), lambda b_, i: (b_, i, 0)),
            _resident(w_ukv.shape),
            _resident((1, MXU_COLS)),
            _resident(bd.shape),
            _resident(tile.shape),
        ],
        out_specs=specs,
        compiler_params=_params("parallel", "parallel"),
        name="mla_expand",
    )(ckv, kpe, w_ukv, _tile_gain(k_nope_g), bd, tile)


def _attn_kernel(*refs, mode, ctx, mla, sink, tq, rows, pp, group, bounded):
    it = iter(refs)
    q_ref, k_ref, v_ref = next(it), next(it), next(it)
    kc_ref, vc_ref = (next(it), next(it)) if ctx else (None, None)
    q2_ref, k2_ref = (next(it), next(it)) if mla else (None, None)
    kc2_ref = next(it) if (mla and ctx) else None
    sink_ref = next(it) if sink else None
    tbl_ref = next(it) if mode == "na" else None
    bound_ref = next(it) if bounded else None
    o_ref = next(it)

    g = pl.program_id(1)
    t = pl.program_id(2)
    lane = lax.broadcasted_iota(jnp.int32, (1, LANES), 1)
    plans = [_attn_pair_plan(i, g, t, lane, q_ref, k_ref, v_ref, kc_ref, vc_ref, q2_ref, k2_ref, kc2_ref,
                             sink_ref, tbl_ref, mode=mode, ctx=ctx, mla=mla, sink=sink, tq=tq, rows=rows,
                             pp=pp, group=group)
             for i in range(pp)]
    floor = bound_ref[0] if bounded else None

    def scores(i, j):
        kk, _, fix, _ = plans[i][1][j]
        s = _dot(kk, plans[i][0])
        return s if fix is None else fix(s)

    n_chunks = len(plans[0][1])
    state = [None] * pp
    s_next = [scores(i, 0) for i in range(pp)]
    for j in range(n_chunks):
        for i in range(pp):
            s = s_next[i]
            if j + 1 < n_chunks:
                s_next[i] = scores(i, j + 1)
            _, vv, _, exact_max = plans[i][1][j]
            state[i] = _softmax_step(state[i], s, vv, plans[i][2], floor, exact_max or not bounded)
    for i in range(pp):
        m, den, acc = state[i]
        if sink:
            den = den + jnp.exp2(plans[i][2] - m)
        o = acc / den
        o_pair = jnp.concatenate([o[:HEAD_DIM, :tq], o[HEAD_DIM:, tq:]], axis=0)
        o_ref[0, :, i * LANES:(i + 1) * LANES] = o_pair.T.astype(o_ref.dtype)


def _softmax_step(state, s, vv, sk, floor, exact_max):
    def probs(m):
        pr = jnp.exp2(s - m)
        return pr.astype(BF16), jnp.sum(pr, axis=0, keepdims=True)

    if state is None:
        m = None
        if floor is not None:
            m = jnp.full((1, s.shape[1]), floor, F32)
        if sk is not None:
            m = sk if m is None else jnp.maximum(m, sk)
        if exact_max:
            m_c = jnp.max(s, axis=0, keepdims=True)
            m = m_c if m is None else jnp.maximum(m, m_c)
        pr, total = probs(m)
        return m, total, _dot_tn(vv, pr)
    m, den, acc = state
    if exact_max:
        m_new = jnp.maximum(m, jnp.max(s, axis=0, keepdims=True))
        alpha = jnp.exp2(m - m_new)
        pr, total = probs(m_new)
        return m_new, alpha * den + total, alpha * acc + _dot_tn(vv, pr)
    pr, total = probs(m)
    return m, den + total, acc + _dot_tn(vv, pr)


def _attn_pair_plan(i, g, t, lane, q_ref, k_ref, v_ref, kc_ref, vc_ref, q2_ref, k2_ref, kc2_ref, sink_ref,
                    tbl_ref, *, mode, ctx, mla, sink, tq, rows, pp, group):
    ql = slice(i * LANES, (i + 1) * LANES)
    kvi = i // group if pp >= group else 0
    kl = slice(kvi * LANES, (kvi + 1) * LANES)

    row = lax.broadcasted_iota(jnp.int32, (LANES, 1), 0)

    def stack(x, m0, m1):
        xt = x.astype(F32).T
        return jnp.concatenate([jnp.where(m0, xt, 0.0), jnp.where(m1, xt, 0.0)], axis=1).astype(BF16)

    qs = stack(q_ref[0, :, ql], row < HEAD_DIM, row >= HEAD_DIM)
    if mla:
        if pp == 1:
            q2, base = q2_ref[0], (g % 2) * (2 * MLA_ROPE)
        else:
            q2, base = q2_ref[0, :, (i // 2) * LANES:(i // 2 + 1) * LANES], (i % 2) * (2 * MLA_ROPE)
        m0 = (row >= base) & (row < base + MLA_ROPE)
        m1 = (row >= base + MLA_ROPE) & (row < base + 2 * MLA_ROPE)
        qs = jnp.concatenate([qs, stack(q2, m0, m1)], axis=0)

    def keys(ref, ref2, rows_):
        kk = ref[0, rows_, kl].astype(BF16)
        if mla:
            kk = jnp.concatenate([kk, ref2[0, rows_, :].astype(BF16)], axis=1)
        return kk

    chunks = []
    if ctx:
        lc = kc_ref.shape[1]
        kc_len = min(KEY_CHUNK, lc)
        for c in range(0, lc, kc_len):
            rows_ = slice(c, c + kc_len)
            chunks.append((keys(kc_ref, kc2_ref, rows_), vc_ref[0, rows_, kl].astype(BF16), None, True))
    if mode == "full":
        lk = k_ref.shape[1]
        kc_len = min(KEY_CHUNK, lk)
        for c in range(0, lk, kc_len):
            rows_ = slice(c, c + kc_len)
            chunks.append((keys(k_ref, k2_ref, rows_), v_ref[0, rows_, kl].astype(BF16), None, False))
    elif mode == "swa":
        win = tq + 2 * SWA_WINDOW
        ws = pl.multiple_of(jnp.clip(t * tq - SWA_WINDOW, 0, k_ref.shape[1] - win), SWA_WINDOW)

        def band(s):
            kpos = ws + lax.broadcasted_iota(jnp.int32, (win, 1), 0)
            qi = t * tq + lax.broadcasted_iota(jnp.int32, (1, tq), 1)
            qpos = jnp.concatenate([qi, qi], axis=1)
            return jnp.where(jnp.abs(kpos - qpos) <= SWA_WINDOW, s, NEG_INF)

        chunks.append((k_ref[0, pl.ds(ws, win), kl], v_ref[0, pl.ds(ws, win), kl], band, False))
    else:
        qrows = tq // GRID_W
        wrows = qrows + NA_WIN_ROWS
        r0 = t * qrows
        ws_row = jnp.clip(r0 - NA_WIN_ROWS // 2, 0, rows - wrows)
        ws = pl.multiple_of(ws_row * GRID_W, GRID_W)

        def window_bias(s):
            e0 = ws_row - r0 + (NA_WIN_ROWS - 1) + qrows - 1
            out_rows = []
            for j in range(wrows):
                kr = ws_row + j
                blocks = []
                for hh in range(2):
                    for ii in range(qrows // 2):
                        ok = []
                        for qr in (r0 + 2 * ii, r0 + 2 * ii + 1):
                            rs = jnp.clip(qr - NA_WIN_ROWS // 2, 0, rows - NA_WIN_ROWS)
                            ok.append(((kr >= rs) & (kr < rs + NA_WIN_ROWS)).astype(jnp.int32))
                        valid = jnp.where(lane < GRID_W, ok[0], ok[1]) > 0
                        lo = hh * tq + ii * LANES
                        blk = (s[j * GRID_W:(j + 1) * GRID_W, lo:lo + LANES]
                               + tbl_ref[2 * i + hh, e0 + j - 2 * ii])
                        blocks.append(jnp.where(valid, blk, NEG_INF))
                out_rows.append(jnp.concatenate(blocks, axis=1))
            return jnp.concatenate(out_rows, axis=0)

        chunks.append((k_ref[0, pl.ds(ws, wrows * GRID_W), kl], v_ref[0, pl.ds(ws, wrows * GRID_W), kl],
                       window_bias, False))

    sk = None
    if sink:
        first = lax.broadcasted_iota(jnp.int32, (1, 2 * tq), 1) < tq
        head = 2 * (g * pp + i)
        sk = jnp.where(first, sink_ref[head], sink_ref[head + 1])
    return qs, chunks, sk


def _attention(q, k, v, *, mode="full", kc=None, vc=None, q2=None, k2=None, kc2=None, sink=None,
               na_table=None, bound=None, pp=1, q_tile=Q_TILE):
    b, lq, nq = q.shape
    lk = k.shape[1]
    pairs = nq // LANES
    group = pairs // (k.shape[2] // LANES)
    tq = min(q_tile, lq)
    ctx, mla = kc is not None, q2 is not None
    assert pairs % pp == 0 and (pp % group == 0 or group % pp == 0)
    assert not mla or pp == 1 or pp % 2 == 0

    kv_lanes = LANES * max(1, pp // group)
    qmap = lambda b_, g, t: (b_, t, g)
    if pp >= group:
        kvmap = lambda b_, g, t: (b_, 0, g)
    else:
        kvmap = lambda b_, g, t: (b_, 0, (g * pp) // group)
    shared = lambda b_, g, t: (b_, 0, 0)
    args = [q, k, v]
    in_specs = [pl.BlockSpec((1, tq, pp * LANES), qmap),
                pl.BlockSpec((1, lk, kv_lanes), kvmap),
                pl.BlockSpec((1, lk, kv_lanes), kvmap)]
    if ctx:
        lc = kc.shape[1]
        args += [kc, vc]
        in_specs += [pl.BlockSpec((1, lc, kv_lanes), kvmap)] * 2
    if mla:
        args += [q2, k2]
        if pp == 1:
            in_specs += [pl.BlockSpec((1, tq, LANES), lambda b_, g, t: (b_, t, g // 2))]
        else:
            in_specs += [pl.BlockSpec((1, tq, pp * LANES // 2), qmap)]
        in_specs += [pl.BlockSpec((1, lk, LANES), shared)]
        if ctx:
            args += [kc2]
            in_specs += [pl.BlockSpec((1, kc2.shape[1], LANES), shared)]
    if sink is not None:
        args += [sink]
        in_specs += [pl.BlockSpec(memory_space=pltpu.SMEM)]
    if mode == "na":
        args += [na_table]
        in_specs += [pl.BlockSpec((2 * pp,) + na_table.shape[1:], lambda b_, g, t: (g, 0, 0, 0))]
    if bound is not None:
        args += [bound]
        in_specs += [pl.BlockSpec(memory_space=pltpu.SMEM)]
    return pl.pallas_call(
        functools.partial(_attn_kernel, mode=mode, ctx=ctx, mla=mla, sink=sink is not None, tq=tq,
                          rows=lq // GRID_W, pp=pp, group=group, bounded=bound is not None),
        out_shape=jax.ShapeDtypeStruct((b, lq, nq), BF16),
        grid=(b, pairs // pp, lq // tq),
        in_specs=in_specs,
        out_specs=pl.BlockSpec((1, tq, pp * LANES), qmap),
        compiler_params=_params("parallel", "parallel", "parallel"),
        name="attention_" + mode,
    )(*args)


def _attention_guarded(score_bound, *args, **kwargs):
    bound = jnp.reshape(score_bound * BOUND_SLACK, (1,)).astype(F32)
    return lax.cond(bound[0] <= MAX_SCORE_BOUND,
                    lambda: _attention(*args, bound=bound, **kwargs),
                    lambda: _attention(*args, **kwargs))


def _out_ffn_kernel(x_ref, a_ref, mod_ref, g_ref, wo_ref, wg_ref, wu_ref, wd_ref, o_ref, *, chunks):
    mod = lambda r: mod_ref[0, r:r + 1, :]
    x1 = x_ref[0] + mod(2) * _dot(a_ref[0], wo_ref[...])
    h = _norm_mod(x1, g_ref[...], mod(3), mod(4)).astype(BF16)
    hidden = wg_ref.shape[1]
    step = hidden // chunks
    gate_up = lambda c: (_dot(h, wg_ref[:, c:c + step]), _dot(h, wu_ref[:, c:c + step]))
    y = None
    nxt = gate_up(0)
    for c in range(0, hidden, step):
        gate, up = nxt
        if c + step < hidden:
            nxt = gate_up(c + step)
        act = (gate * jax.nn.sigmoid(gate) * up).astype(BF16)
        part = _dot(act, wd_ref[c:c + step, :])
        y = part if y is None else y + part
    o_ref[0] = x1 + mod(5) * y


def _out_ffn(x, attn, mod, g2, w_o, w_gate, w_up, w_down, layer):
    b, l, d = x.shape
    tl = min(TOKEN_TILE, l)
    tok = lambda n: pl.BlockSpec((1, tl, n), lambda b_, i: (b_, i, 0))
    of_layer = lambda w: pl.BlockSpec((None,) + w.shape[1:], lambda b_, i: (layer, 0, 0),
                                      pipeline_mode=pl.Buffered(1))
    return pl.pallas_call(
        functools.partial(_out_ffn_kernel, chunks=11),
        out_shape=jax.ShapeDtypeStruct((b, l, d), F32),
        grid=(b, l // tl),
        in_specs=[tok(d), tok(attn.shape[2]), _mod_spec(mod), _resident((1, d)),
                  _resident(w_o.shape), of_layer(w_gate), of_layer(w_up), of_layer(w_down)],
        out_specs=tok(d),
        compiler_params=_params("parallel", "parallel"),
        name="out_ffn",
    )(x, attn, mod, g2.reshape(1, d), w_o, w_gate, w_up, w_down)


def _max_abs(a):
    return jnp.max(jnp.abs(a))


def _block_diag_ones(n, group):
    idx = np.arange(n) // group
    return jnp.asarray(idx[:, None] == idx[None, :], BF16)


def _tile_gain(g, scale=1.0):
    return jnp.tile(g * scale, MXU_COLS // g.shape[0]).reshape(1, MXU_COLS)


def _rope_tables(n_tokens, r):
    quarter = r // 4
    n_rows = n_tokens // GRID_W
    inv_freq = ROPE_THETA ** (-jnp.arange(quarter, dtype=F32) / quarter)
    ang = jnp.arange(max(n_rows, GRID_W), dtype=F32)[:, None] * inv_freq[None, :]
    by_row = lambda a: jnp.broadcast_to(a[:n_rows, None, :], (n_rows, GRID_W, quarter)).reshape(n_tokens, quarter)
    by_col = lambda a: jnp.broadcast_to(a[None, :GRID_W, :], (n_rows, GRID_W, quarter)).reshape(n_tokens, quarter)
    cos, sin = jnp.cos(ang), jnp.sin(ang)
    cos_head = jnp.concatenate([by_row(cos), by_row(cos), by_col(cos), by_col(cos)], axis=-1)
    sin_head = jnp.concatenate([-by_row(sin), by_row(sin), -by_col(sin), by_col(sin)], axis=-1)
    return jnp.tile(cos_head, (1, MXU_COLS // r)), jnp.tile(sin_head, (1, MXU_COLS // r))


_GQA_HEADS = np.array([8 * (p // 4) + 4 * e + (p % 4) for p in range(N_HEADS // 2) for e in range(2)])
_GQA_COLS = (_GQA_HEADS[:, None] * HEAD_DIM + np.arange(HEAD_DIM)[None, :]).reshape(-1)


def _na_bias_table(rpb, q_rows):
    n_heads, _, n_dc = rpb.shape
    qc = np.arange(LANES)[None, :] % GRID_W
    kc = np.arange(GRID_W)[:, None]
    start = np.clip(qc - NA_WIN_COLS // 2, 0, GRID_W - NA_WIN_COLS)
    valid = (kc >= start) & (kc < start + NA_WIN_COLS)
    dcol = np.clip(kc - qc + NA_WIN_COLS - 1, 0, n_dc - 1) + n_dc * (np.arange(LANES)[None, :] // GRID_W)
    onehot = (np.arange(2 * n_dc)[:, None] == dcol.reshape(1, -1)).astype(np.float32)
    n_e = 2 * q_rows + 2 * NA_WIN_ROWS - 2
    rp = jnp.pad(rpb, ((0, 0), (q_rows, q_rows), (0, 0)))
    pairs = jnp.concatenate([rp[:, 1:1 + n_e], rp[:, 0:n_e]], axis=-1)
    t = jnp.dot(pairs.reshape(n_heads * n_e, 2 * n_dc), onehot, precision=lax.Precision.HIGHEST)
    return jnp.where(valid[None, None], t.reshape(n_heads, n_e, GRID_W, LANES), NEG_INF)


def kernel(x_prompt, x_sample, cache_na_k, cache_na_v, cache_swa_k, cache_swa_v, cache_mla_ckv, cache_mla_kpe, cache_gqa_k, cache_gqa_v, c, c_ctx, norm1_g, norm2_g, w_mod, b_mod, na_w_qkv, na_q_g, na_k_g, na_rpb, na_w_o, swa_w_qkv, swa_q_g, swa_k_g, swa_sink, swa_w_o, mla_w_dq, mla_q_lora_g, mla_w_uq, mla_q_nope_g, mla_q_pe_g, mla_w_dkv, mla_kv_lora_g, mla_k_pe_g, mla_w_ukv, mla_k_nope_g, mla_w_o, gqa_w_qkv, gqa_q_g, gqa_k_g, gqa_w_o, ffn_w_gate, ffn_w_up, ffn_w_down):
    depth, d = norm1_g.shape
    bp, lp, _ = x_prompt.shape
    bs, ls, _ = x_sample.shape
    hq, hkv = N_HEADS * HEAD_DIM, N_KV_HEADS * HEAD_DIM
    scale = HEAD_DIM ** -0.5 * LOG2E

    cond = jnp.concatenate([c_ctx[None, :], c, jnp.zeros((16 - 1 - bs, d), F32)], axis=0)
    mods = _adaln(cond, w_mod, b_mod).reshape(depth, 16, 6, d)
    rope64 = _rope_tables(ls, HEAD_DIM)
    rope32 = _rope_tables(ls, MLA_ROPE)
    flat = lambda a, r: a[:, r].reshape(a.shape[0], a.shape[2], -1)

    wg, wu, wd = ffn_w_gate.astype(BF16), ffn_w_up.astype(BF16), ffn_w_down.astype(BF16)
    fold_n = max(1, TOKEN_TILE // lp)
    fold_n = fold_n if bp % fold_n == 0 else 1
    fold = lambda a: a.reshape(a.shape[0] // fold_n, fold_n * a.shape[1], a.shape[2])
    unfold = lambda a: a.reshape(bp, lp, a.shape[2])
    xp, xs = fold(x_prompt), x_sample
    outs = {}
    for li in range(depth):
        kind, r = li % 4, li // 4
        mp, ms = mods[li, 0:1], mods[li, 1:1 + bs]
        g1 = norm1_g[li]
        if kind in (0, 1, 3):
            w_qkv, q_g, k_g, w_o = ((na_w_qkv, na_q_g, na_k_g, na_w_o), (swa_w_qkv, swa_q_g, swa_k_g, swa_w_o),
                                    None, (gqa_w_qkv, gqa_q_g, gqa_k_g, gqa_w_o))[kind]
            w_qkv, w_o = w_qkv[r], w_o[r]
            nk = hq if kind == 0 else hkv
            if kind != 0:
                w_qkv = jnp.concatenate([w_qkv[:, :hq][:, _GQA_COLS], w_qkv[:, hq:]], axis=1)
                w_o = w_o[_GQA_COLS, :]
            w_qkv, w_o = w_qkv.astype(BF16), w_o.astype(BF16)
            qp, kp, vp = map(unfold, _qkv_project(xp, g1, mp, w_qkv, q_g[r] * scale, k_g[r], hq, nk, F32))
            qs, ks, vs = _qkv_project(xs, g1, ms, w_qkv, q_g[r] * scale, k_g[r], hq, nk, BF16,
                                      rope_tables=None if kind == 0 else rope64)
            qk_bound = HEAD_DIM * _max_abs(q_g[r] * scale) * _max_abs(k_g[r])
            if kind == 0:
                kc, vc = flat(cache_na_k, r), flat(cache_na_v, r)
                rpb = na_rpb[r] * LOG2E
                ap = _attention_guarded(qk_bound, qp, kp, vp, pp=PROMPT_PAIRS)
                as_ = _attention_guarded(qk_bound + _max_abs(rpb), qs, ks, vs, mode="na", kc=kc, vc=vc,
                                         na_table=_na_bias_table(rpb, WINDOW_Q_TILE // GRID_W),
                                         pp=WINDOW_PAIRS, q_tile=WINDOW_Q_TILE)
                outs["na_k"], outs["na_v"] = kp, vp
            elif kind == 1:
                kc, vc = flat(cache_swa_k, r), flat(cache_swa_v, r)
                sink = swa_sink[r][_GQA_HEADS] * LOG2E
                ap = _attention_guarded(qk_bound, qp, kp, vp, sink=sink, pp=PROMPT_PAIRS)
                as_ = _attention_guarded(qk_bound, qs, ks, vs, mode="swa", kc=kc, vc=vc, sink=sink,
                                         pp=WINDOW_PAIRS, q_tile=WINDOW_Q_TILE)
                outs["swa_k"], outs["swa_v"] = kp, vp
            else:
                kc, vc = flat(cache_gqa_k, r), flat(cache_gqa_v, r)
                ap = _attention_guarded(qk_bound, qp, kp, vp, pp=PROMPT_PAIRS)
                as_ = _attention_guarded(qk_bound, qs, ks, vs, kc=kc, vc=vc, pp=LATENT_PAIRS)
                outs["gqa_k"], outs["gqa_v"] = kp, vp
        else:
            dk = MLA_NOPE + MLA_ROPE
            w_uq = mla_w_uq[r].reshape(-1, N_HEADS, dk)
            w_uq = jnp.concatenate([w_uq[:, :, :MLA_NOPE].reshape(-1, N_HEADS * MLA_NOPE),
                                    w_uq[:, :, MLA_NOPE:].reshape(-1, N_HEADS * MLA_ROPE)], axis=1).astype(BF16)
            w_ukv = mla_w_ukv[r].reshape(-1, N_HEADS, MLA_NOPE + MLA_V)
            w_ukv = jnp.concatenate([w_ukv[:, :, :MLA_NOPE].reshape(-1, N_HEADS * MLA_NOPE),
                                     w_ukv[:, :, MLA_NOPE:].reshape(-1, N_HEADS * MLA_V)], axis=1).astype(BF16)
            w_dkv = jnp.pad(mla_w_dkv[r], ((0, 0), (0, 2 * LANES - MLA_KV_LORA - MLA_ROPE))).astype(BF16)
            w_dq, w_o = mla_w_dq[r].astype(BF16), mla_w_o[r].astype(BF16)
            qscale = dk ** -0.5 * LOG2E
            proj = functools.partial(
                _mla_project, w_dq=w_dq, q_lora_g=mla_q_lora_g[r], w_uq=w_uq, q_nope_g=mla_q_nope_g[r] * qscale,
                q_pe_g=mla_q_pe_g[r] * qscale, w_dkv=w_dkv, kv_lora_g=mla_kv_lora_g[r], k_pe_g=mla_k_pe_g[r])
            qn_p, qpe_p, ckv_p, kpe_p = proj(xp, g1, mp)
            qn_s, qpe_s, ckv_s, kpe_s = proj(xs, g1, ms, rope_tables=rope32)
            kn_p, v_p, kpt_p = map(unfold, _mla_expand(ckv_p, kpe_p, w_ukv, mla_k_nope_g[r]))
            qn_p, qpe_p, ckv_p, kpe_p = map(unfold, (qn_p, qpe_p, ckv_p, kpe_p))
            kn_s, v_s, kpt_s = _mla_expand(ckv_s, kpe_s, w_ukv, mla_k_nope_g[r])
            kn_c, v_c, kpt_c = _mla_expand(cache_mla_ckv[:, r], cache_mla_kpe[:, r], w_ukv, mla_k_nope_g[r])
            q_norm = jnp.sqrt(MLA_NOPE * _max_abs(mla_q_nope_g[r] * qscale) ** 2
                              + MLA_ROPE * _max_abs(mla_q_pe_g[r] * qscale) ** 2)
            k_norm = jnp.sqrt(MLA_NOPE * _max_abs(mla_k_nope_g[r]) ** 2 + MLA_ROPE * _max_abs(mla_k_pe_g[r]) ** 2)
            ap = _attention_guarded(q_norm * k_norm, qn_p, kn_p, v_p, q2=qpe_p, k2=kpt_p, pp=PROMPT_PAIRS)
            as_ = _attention_guarded(q_norm * k_norm, qn_s, kn_s, v_s, kc=kn_c, vc=v_c, q2=qpe_s, k2=kpt_s,
                                     kc2=kpt_c, pp=LATENT_PAIRS)
            outs["mla_ckv"], outs["mla_kpe"] = ckv_p, kpe_p[:, :, :MLA_ROPE]
        xp = _out_ffn(xp, fold(ap), mp, norm2_g[li], w_o, wg, wu, wd, li)
        xs = _out_ffn(xs, as_, ms, norm2_g[li], w_o, wg, wu, wd, li)

    heads = lambda a, h: a.reshape(bp, 1, lp, h, HEAD_DIM)
    return (unfold(xp), xs, heads(outs["na_k"], N_HEADS), heads(outs["na_v"], N_HEADS),
            heads(outs["swa_k"], N_KV_HEADS), heads(outs["swa_v"], N_KV_HEADS),
            outs["mla_ckv"][:, None], outs["mla_kpe"][:, None],
            heads(outs["gqa_k"], N_KV_HEADS), heads(outs["gqa_v"], N_KV_HEADS))
```

```python
import functools

import numpy as np
import jax
import jax.numpy as jnp
from jax import lax
from jax.experimental import pallas as pl
from jax.experimental.pallas import tpu as pltpu

GRID_W = 64
HEAD_DIM = 64
N_HEADS = 16
N_KV_HEADS = 4
NA_WIN_ROWS = 8
NA_WIN_COLS = 16
SWA_WINDOW = 128
MLA_KV_LORA = 128
MLA_NOPE = 64
MLA_ROPE = 32
MLA_V = 64
ROPE_THETA = 10000.0
EPS = 1e-6
NEG_INF = -1e30

LANES = 128
MXU_COLS = 256
VMEM_LIMIT = 56 * 1024 * 1024
TOKEN_TILE = 512
Q_TILE = 256
KEY_CHUNK = 256
LOG2E = 1.4426950408889634
PROMPT_PAIRS = 8
LATENT_PAIRS = 4
WINDOW_Q_TILE = 128
WINDOW_PAIRS = 8
BOUND_SLACK = 1.01
MAX_SCORE_BOUND = 48.0

F32 = jnp.float32
BF16 = jnp.bfloat16


def _params(*sem):
    return pltpu.CompilerParams(dimension_semantics=sem, vmem_limit_bytes=VMEM_LIMIT)


def _resident(shape):
    nd = len(shape)
    return pl.BlockSpec(shape, lambda *_: (0,) * nd, pipeline_mode=pl.Buffered(1))


def _dot(a, b):
    return jnp.dot(a, b, preferred_element_type=F32)


def _dot_nt(a, b):
    return lax.dot_general(a, b, (((1,), (1,)), ((), ())), preferred_element_type=F32)


def _dot_tn(a, b):
    return lax.dot_general(a, b, (((0,), (0,)), ((), ())), preferred_element_type=F32)


def _norm_mod(x, g, shift, scale):
    ms = jnp.mean(x * x, axis=-1, keepdims=True)
    y = x * lax.rsqrt(ms + EPS) * g
    return y * (1.0 + scale) + shift


def _row_rms(x, g, n):
    ms = jnp.sum(x * x, axis=-1, keepdims=True) * (1.0 / n)
    return x * lax.rsqrt(ms + EPS) * g


def _group_rms(y, gain, bd, group):
    ss = _dot((y * y).astype(BF16), bd)
    return y * lax.rsqrt(ss * (1.0 / group) + EPS) * gain


def _pipelined_blocks(project, jobs):
    y_next = project(jobs[0][0])
    for j, (_, ref, col, finish) in enumerate(jobs):
        y = y_next
        if j + 1 < len(jobs):
            y_next = project(jobs[j + 1][0])
        ref[0, :, col:col + MXU_COLS] = finish(y).astype(ref.dtype)


def _rope(y, cos, sin_signed, quarter):
    n = y.shape[-1]
    lane = lax.broadcasted_iota(jnp.int32, (1, n), 1)
    first = (lane & quarter) == 0
    partner = jnp.where(first, pltpu.roll(y, n - quarter, 1), pltpu.roll(y, quarter, 1))
    return y * cos + partner * sin_signed


def _adaln_kernel(cond_ref, w_ref, b_ref, o_ref):
    c = cond_ref[...]
    a = (c * jax.nn.sigmoid(c)).astype(BF16)
    o_ref[0] = _dot(a, w_ref[0].astype(BF16)) + b_ref[0]


def _adaln(cond, w_mod, b_mod):
    depth, d, n = w_mod.shape
    rows = cond.shape[0]
    tn = 1536
    return pl.pallas_call(
        _adaln_kernel,
        out_shape=jax.ShapeDtypeStruct((depth, rows, n), F32),
        grid=(depth, n // tn),
        in_specs=[
            pl.BlockSpec((rows, d), lambda l, j: (0, 0)),
            pl.BlockSpec((1, d, tn), lambda l, j: (l, 0, j)),
            pl.BlockSpec((1, 1, tn), lambda l, j: (l, 0, j)),
        ],
        out_specs=pl.BlockSpec((1, rows, tn), lambda l, j: (l, 0, j)),
        compiler_params=_params("parallel", "parallel"),
        name="adaln",
    )(cond, w_mod, b_mod.reshape(depth, 1, n))


def _mod_spec(mod):
    if mod.shape[0] == 1:
        return pl.BlockSpec((1,) + mod.shape[1:], lambda b, i: (0, 0, 0))
    return pl.BlockSpec((1,) + mod.shape[1:], lambda b, i: (b, 0, 0))


def _qkv_kernel(*refs, nq, nk, rope):
    x_ref, g_ref, mod_ref, w_ref, qg_ref, kg_ref, bd_ref = refs[:7]
    if rope:
        cos, sin = refs[7][...], refs[8][...]
        q_ref, k_ref, v_ref = refs[9:]
    else:
        cos = sin = None
        q_ref, k_ref, v_ref = refs[7:]
    h = _norm_mod(x_ref[0], g_ref[...], mod_ref[0, 0:1, :], mod_ref[0, 1:2, :]).astype(BF16)
    bd = bd_ref[...]

    def head(y, gain):
        y = _group_rms(y, gain, bd, HEAD_DIM)
        return _rope(y, cos, sin, HEAD_DIM // 4) if rope else y

    jobs = [(c, q_ref, c, lambda y: head(y, qg_ref[...])) for c in range(0, nq, MXU_COLS)]
    jobs += [(nq + c, k_ref, c, lambda y: head(y, kg_ref[...])) for c in range(0, nk, MXU_COLS)]
    jobs += [(nq + nk + c, v_ref, c, lambda y: y) for c in range(0, nk, MXU_COLS)]
    _pipelined_blocks(lambda col: _dot(h, w_ref[:, col:col + MXU_COLS]), jobs)


def _qkv_project(x, g, mod, w, q_gain, k_gain, nq, nk, kv_dtype, rope_tables=None):
    b, l, d = x.shape
    tl = min(TOKEN_TILE, l)
    rope = rope_tables is not None
    bd = _block_diag_ones(MXU_COLS, HEAD_DIM)
    args = [x, g.reshape(1, d), mod, w, _tile_gain(q_gain), _tile_gain(k_gain), bd]
    in_specs = [
        pl.BlockSpec((1, tl, d), lambda b_, i: (b_, i, 0)),
        _resident((1, d)),
        _mod_spec(mod),
        _resident(w.shape),
        _resident((1, MXU_COLS)),
        _resident((1, MXU_COLS)),
        _resident(bd.shape),
    ]
    if rope:
        args += list(rope_tables)
        in_specs += [pl.BlockSpec((tl, MXU_COLS), lambda b_, i: (i, 0))] * 2
    out = lambda n, dt: (jax.ShapeDtypeStruct((b, l, n), dt),
                         pl.BlockSpec((1, tl, n), lambda b_, i: (b_, i, 0)))
    shapes, specs = zip(out(nq, BF16), out(nk, kv_dtype), out(nk, kv_dtype))
    return pl.pallas_call(
        functools.partial(_qkv_kernel, nq=nq, nk=nk, rope=rope),
        out_shape=shapes,
        grid=(b, l // tl),
        in_specs=in_specs,
        out_specs=specs,
        compiler_params=_params("parallel", "parallel"),
        name="qkv_project",
    )(*args)


def _mla_project_kernel(*refs, rope):
    (x_ref, g_ref, mod_ref, wdq_ref, qlg_ref, wuq_ref, qng_ref, qpg_ref, bdn_ref, bdp_ref,
     wdkv_ref, kvg_ref, kpg_ref) = refs[:13]
    if rope:
        cos, sin = refs[13][...], refs[14][...]
        qn_ref, qp_ref, ckv_ref, kpe_ref = refs[15:]
    else:
        cos = sin = None
        qn_ref, qp_ref, ckv_ref, kpe_ref = refs[13:]
    h = _norm_mod(x_ref[0], g_ref[...], mod_ref[0, 0:1, :], mod_ref[0, 1:2, :]).astype(BF16)
    q_lora = wdq_ref.shape[1]
    ql = _row_rms(_dot(h, wdq_ref[...]), qlg_ref[...], q_lora).astype(BF16)
    n_nope = qn_ref.shape[2]
    n_pe = qp_ref.shape[2]
    def pe_head(y):
        y = _group_rms(y, qpg_ref[...], bdp_ref[...], MLA_ROPE)
        return _rope(y, cos, sin, MLA_ROPE // 4) if rope else y

    jobs = [(c, qn_ref, c, lambda y: _group_rms(y, qng_ref[...], bdn_ref[...], MLA_NOPE))
            for c in range(0, n_nope, MXU_COLS)]
    jobs += [(n_nope + c, qp_ref, c, pe_head) for c in range(0, n_pe, MXU_COLS)]
    _pipelined_blocks(lambda col: _dot(ql, wuq_ref[:, col:col + MXU_COLS]), jobs)
    ckv_kpe = _dot(h, wdkv_ref[...])
    ckv_ref[0] = _row_rms(ckv_kpe[:, :MLA_KV_LORA], kvg_ref[...], MLA_KV_LORA)
    kpe = _row_rms(ckv_kpe[:, MLA_KV_LORA:], kpg_ref[...], MLA_ROPE)
    if rope:
        kpe = _rope(kpe, cos[:, :LANES], sin[:, :LANES], MLA_ROPE // 4)
    kpe_ref[0] = kpe


def _mla_project(x, g, mod, w_dq, q_lora_g, w_uq, q_nope_g, q_pe_g, w_dkv, kv_lora_g, k_pe_g,
                 rope_tables=None):
    b, l, d = x.shape
    tl = min(TOKEN_TILE, l)
    rope = rope_tables is not None
    n_nope, n_pe = N_HEADS * MLA_NOPE, N_HEADS * MLA_ROPE
    bdn = _block_diag_ones(MXU_COLS, MLA_NOPE)
    bdp = _block_diag_ones(MXU_COLS, MLA_ROPE)
    kpg = jnp.pad(k_pe_g, (0, LANES - MLA_ROPE)).reshape(1, LANES)
    args = [x, g.reshape(1, d), mod, w_dq, q_lora_g.reshape(1, -1), w_uq, _tile_gain(q_nope_g),
            _tile_gain(q_pe_g), bdn, bdp, w_dkv, kv_lora_g.reshape(1, -1), kpg]
    in_specs = [pl.BlockSpec((1, tl, d), lambda b_, i: (b_, i, 0)), _resident((1, d)), _mod_spec(mod)]
    in_specs += [_resident(a.shape) for a in args[3:]]
    if rope:
        args += list(rope_tables)
        in_specs += [pl.BlockSpec((tl, MXU_COLS), lambda b_, i: (i, 0))] * 2
    out = lambda n, dt: (jax.ShapeDtypeStruct((b, l, n), dt),
                         pl.BlockSpec((1, tl, n), lambda b_, i: (b_, i, 0)))
    shapes, specs = zip(out(n_nope, BF16), out(n_pe, BF16), out(MLA_KV_LORA, F32), out(LANES, F32))
    return pl.pallas_call(
        functools.partial(_mla_project_kernel, rope=rope),
        out_shape=shapes,
        grid=(b, l // tl),
        in_specs=in_specs,
        out_specs=specs,
        compiler_params=_params("parallel", "parallel"),
        name="mla_project",
    )(*args)


def _mla_expand_kernel(ckv_ref, kpe_ref, w_ref, kng_ref, bd_ref, tile_ref, kn_ref, v_ref, kpt_ref):
    c = ckv_ref[0].astype(BF16)
    n = kn_ref.shape[2]
    jobs = [(col, kn_ref, col, lambda y: _group_rms(y, kng_ref[...], bd_ref[...], MLA_NOPE))
            for col in range(0, n, MXU_COLS)]
    jobs += [(n + col, v_ref, col, lambda y: y) for col in range(0, n, MXU_COLS)]
    _pipelined_blocks(lambda col: _dot(c, w_ref[:, col:col + MXU_COLS]), jobs)
    kpt_ref[0] = _dot(kpe_ref[0].astype(BF16), tile_ref[...]).astype(kpt_ref.dtype)


def _mla_expand(ckv, kpe, w_ukv, k_nope_g):
    b, l, _ = ckv.shape
    tl = min(TOKEN_TILE, l)
    n = N_HEADS * MLA_NOPE
    pw = kpe.shape[2]
    bd = _block_diag_ones(MXU_COLS, MLA_NOPE)
    tile = np.zeros((pw, LANES), np.float32)
    for r in range(LANES // MLA_ROPE):
        tile[np.arange(MLA_ROPE), r * MLA_ROPE + np.arange(MLA_ROPE)] = 1.0
    tile = jnp.asarray(tile, BF16)
    out = lambda m: (jax.ShapeDtypeStruct((b, l, m), BF16), pl.BlockSpec((1, tl, m), lambda b_, i: (b_, i, 0)))
    shapes, specs = zip(out(n), out(n), out(LANES))
    return pl.pallas_call(
        _mla_expand_kernel,
        out_shape=shapes,
        grid=(b, l // tl),
        in_specs=[
            pl.BlockSpec((1, tl, MLA_KV_LORA), lambda b_, i: (b_, i, 0)),
            pl.BlockSpec((1, tl, pw), lambda b_, i: (b_, i, 0)),
            _resident(w_ukv.shape),
            _resident((1, MXU_COLS)),
            _resident(bd.shape),
            _resident(tile.shape),
        ],
        out_specs=specs,
        compiler_params=_params("parallel", "parallel"),
        name="mla_expand",
    )(ckv, kpe, w_ukv, _tile_gain(k_nope_g), bd, tile)


def _attn_kernel(*refs, mode, ctx, mla, sink, tq, rows, pp, group, bounded):
    it = iter(refs)
    q_ref, k_ref, v_ref = next(it), next(it), next(it)
    kc_ref, vc_ref = (next(it), next(it)) if ctx else (None, None)
    q2_ref, k2_ref = (next(it), next(it)) if mla else (None, None)
    kc2_ref = next(it) if (mla and ctx) else None
    sink_ref = next(it) if sink else None
    tbl_ref = next(it) if mode == "na" else None
    bound_ref = next(it) if bounded else None
    o_ref = next(it)

    g = pl.program_id(1)
    t = pl.program_id(2)
    lane = lax.broadcasted_iota(jnp.int32, (1, LANES), 1)
    plans = [_attn_pair_plan(i, g, t, lane, q_ref, k_ref, v_ref, kc_ref, vc_ref, q2_ref, k2_ref, kc2_ref,
                             sink_ref, tbl_ref, mode=mode, ctx=ctx, mla=mla, sink=sink, tq=tq, rows=rows,
                             pp=pp, group=group)
             for i in range(pp)]
    floor = bound_ref[0] if bounded else None

    def scores(i, j):
        kk, _, fix, _ = plans[i][1][j]
        s = _dot(kk, plans[i][0])
        return s if fix is None else fix(s)

    n_chunks = len(plans[0][1])
    state = [None] * pp
    s_next = [scores(i, 0) for i in range(pp)]
    for j in range(n_chunks):
        for i in range(pp):
            s = s_next[i]
            if j + 1 < n_chunks:
                s_next[i] = scores(i, j + 1)
            _, vv, _, exact_max = plans[i][1][j]
            state[i] = _softmax_step(state[i], s, vv, plans[i][2], floor, exact_max or not bounded)
    for i in range(pp):
        m, den, acc = state[i]
        if sink:
            den = den + jnp.exp2(plans[i][2] - m)
        o = acc / den
        o_pair = jnp.concatenate([o[:HEAD_DIM, :tq], o[HEAD_DIM:, tq:]], axis=0)
        o_ref[0, :, i * LANES:(i + 1) * LANES] = o_pair.T.astype(o_ref.dtype)


def _softmax_step(state, s, vv, sk, floor, exact_max):
    def probs(m):
        pr = jnp.exp2(s - m)
        return pr.astype(BF16), jnp.sum(pr, axis=0, keepdims=True)

    if state is None:
        m = None
        if floor is not None:
            m = jnp.full((1, s.shape[1]), floor, F32)
        if sk is not None:
            m = sk if m is None else jnp.maximum(m, sk)
        if exact_max:
            m_c = jnp.max(s, axis=0, keepdims=True)
            m = m_c if m is None else jnp.maximum(m, m_c)
        pr, total = probs(m)
        return m, total, _dot_tn(vv, pr)
    m, den, acc = state
    if exact_max:
        m_new = jnp.maximum(m, jnp.max(s, axis=0, keepdims=True))
        alpha = jnp.exp2(m - m_new)
        pr, total = probs(m_new)
        return m_new, alpha * den + total, alpha * acc + _dot_tn(vv, pr)
    pr, total = probs(m)
    return m, den + total, acc + _dot_tn(vv, pr)


def _attn_pair_plan(i, g, t, lane, q_ref, k_ref, v_ref, kc_ref, vc_ref, q2_ref, k2_ref, kc2_ref, sink_ref,
                    tbl_ref, *, mode, ctx, mla, sink, tq, rows, pp, group):
    ql = slice(i * LANES, (i + 1) * LANES)
    kvi = i // group if pp >= group else 0
    kl = slice(kvi * LANES, (kvi + 1) * LANES)

    row = lax.broadcasted_iota(jnp.int32, (LANES, 1), 0)

    def stack(x, m0, m1):
        xt = x.astype(F32).T
        return jnp.concatenate([jnp.where(m0, xt, 0.0), jnp.where(m1, xt, 0.0)], axis=1).astype(BF16)

    qs = stack(q_ref[0, :, ql], row < HEAD_DIM, row >= HEAD_DIM)
    if mla:
        if pp == 1:
            q2, base = q2_ref[0], (g % 2) * (2 * MLA_ROPE)
        else:
            q2, base = q2_ref[0, :, (i // 2) * LANES:(i // 2 + 1) * LANES], (i % 2) * (2 * MLA_ROPE)
        m0 = (row >= base) & (row < base + MLA_ROPE)
        m1 = (row >= base + MLA_ROPE) & (row < base + 2 * MLA_ROPE)
        qs = jnp.concatenate([qs, stack(q2, m0, m1)], axis=0)

    def keys(ref, ref2, rows_):
        kk = ref[0, rows_, kl].astype(BF16)
        if mla:
            kk = jnp.concatenate([kk, ref2[0, rows_, :].astype(BF16)], axis=1)
        return kk

    chunks = []
    if ctx:
        lc = kc_ref.shape[1]
        kc_len = min(KEY_CHUNK, lc)
        for c in range(0, lc, kc_len):
            rows_ = slice(c, c + kc_len)
            chunks.append((keys(kc_ref, kc2_ref, rows_), vc_ref[0, rows_, kl].astype(BF16), None, True))
    if mode == "full":
        lk = k_ref.shape[1]
        kc_len = min(KEY_CHUNK, lk)
        for c in range(0, lk, kc_len):
            rows_ = slice(c, c + kc_len)
            chunks.append((keys(k_ref, k2_ref, rows_), v_ref[0, rows_, kl].astype(BF16), None, False))
    elif mode == "swa":
        win = tq + 2 * SWA_WINDOW
        ws = pl.multiple_of(jnp.clip(t * tq - SWA_WINDOW, 0, k_ref.shape[1] - win), SWA_WINDOW)

        def band(s):
            kpos = ws + lax.broadcasted_iota(jnp.int32, (win, 1), 0)
            qi = t * tq + lax.broadcasted_iota(jnp.int32, (1, tq), 1)
            qpos = jnp.concatenate([qi, qi], axis=1)
            return jnp.where(jnp.abs(kpos - qpos) <= SWA_WINDOW, s, NEG_INF)

        chunks.append((k_ref[0, pl.ds(ws, win), kl], v_ref[0, pl.ds(ws, win), kl], band, False))
    else:
        qrows = tq // GRID_W
        wrows = qrows + NA_WIN_ROWS
        r0 = t * qrows
        ws_row = jnp.clip(r0 - NA_WIN_ROWS // 2, 0, rows - wrows)
        ws = pl.multiple_of(ws_row * GRID_W, GRID_W)

        def window_bias(s):
            e0 = ws_row - r0 + (NA_WIN_ROWS - 1) + qrows - 1
            out_rows = []
            for j in range(wrows):
                kr = ws_row + j
                blocks = []
                for hh in range(2):
                    for ii in range(qrows // 2):
                        ok = []
                        for qr in (r0 + 2 * ii, r0 + 2 * ii + 1):
                            rs = jnp.clip(qr - NA_WIN_ROWS // 2, 0, rows - NA_WIN_ROWS)
                            ok.append(((kr >= rs) & (kr < rs + NA_WIN_ROWS)).astype(jnp.int32))
                        valid = jnp.where(lane < GRID_W, ok[0], ok[1]) > 0
                        lo = hh * tq + ii * LANES
                        blk = (s[j * GRID_W:(j + 1) * GRID_W, lo:lo + LANES]
                               + tbl_ref[2 * i + hh, e0 + j - 2 * ii])
                        blocks.append(jnp.where(valid, blk, NEG_INF))
                out_rows.append(jnp.concatenate(blocks, axis=1))
            return jnp.concatenate(out_rows, axis=0)

        chunks.append((k_ref[0, pl.ds(ws, wrows * GRID_W), kl], v_ref[0, pl.ds(ws, wrows * GRID_W), kl],
                       window_bias, False))

    sk = None
    if sink:
        first = lax.broadcasted_iota(jnp.int32, (1, 2 * tq), 1) < tq
        head = 2 * (g * pp + i)
        sk = jnp.where(first, sink_ref[head], sink_ref[head + 1])
    return qs, chunks, sk


def _attention(q, k, v, *, mode="full", kc=None, vc=None, q2=None, k2=None, kc2=None, sink=None,
               na_table=None, bound=None, pp=1, q_tile=Q_TILE):
    b, lq, nq = q.shape
    lk = k.shape[1]
    pairs = nq // LANES
    group = pairs // (k.shape[2] // LANES)
    tq = min(q_tile, lq)
    ctx, mla = kc is not None, q2 is not None
    assert pairs % pp == 0 and (pp % group == 0 or group % pp == 0)
    assert not mla or pp == 1 or pp % 2 == 0

    kv_lanes = LANES * max(1, pp // group)
    qmap = lambda b_, g, t: (b_, t, g)
    if pp >= group:
        kvmap = lambda b_, g, t: (b_, 0, g)
    else:
        kvmap = lambda b_, g, t: (b_, 0, (g * pp) // group)
    shared = lambda b_, g, t: (b_, 0, 0)
    args = [q, k, v]
    in_specs = [pl.BlockSpec((1, tq, pp * LANES), qmap),
                pl.BlockSpec((1, lk, kv_lanes), kvmap),
                pl.BlockSpec((1, lk, kv_lanes), kvmap)]
    if ctx:
        lc = kc.shape[1]
        args += [kc, vc]
        in_specs += [pl.BlockSpec((1, lc, kv_lanes), kvmap)] * 2
    if mla:
        args += [q2, k2]
        if pp == 1:
            in_specs += [pl.BlockSpec((1, tq, LANES), lambda b_, g, t: (b_, t, g // 2))]
        else:
            in_specs += [pl.BlockSpec((1, tq, pp * LANES // 2), qmap)]
        in_specs += [pl.BlockSpec((1, lk, LANES), shared)]
        if ctx:
            args += [kc2]
            in_specs += [pl.BlockSpec((1, kc2.shape[1], LANES), shared)]
    if sink is not None:
        args += [sink]
        in_specs += [pl.BlockSpec(memory_space=pltpu.SMEM)]
    if mode == "na":
        args += [na_table]
        in_specs += [pl.BlockSpec((2 * pp,) + na_table.shape[1:], lambda b_, g, t: (g, 0, 0, 0))]
    if bound is not None:
        args += [bound]
        in_specs += [pl.BlockSpec(memory_space=pltpu.SMEM)]
    return pl.pallas_call(
        functools.partial(_attn_kernel, mode=mode, ctx=ctx, mla=mla, sink=sink is not None, tq=tq,
                          rows=lq // GRID_W, pp=pp, group=group, bounded=bound is not None),
        out_shape=jax.ShapeDtypeStruct((b, lq, nq), BF16),
        grid=(b, pairs // pp, lq // tq),
        in_specs=in_specs,
        out_specs=pl.BlockSpec((1, tq, pp * LANES), qmap),
        compiler_params=_params("parallel", "parallel", "parallel"),
        name="attention_" + mode,
    )(*args)


def _attention_guarded(score_bound, *args, **kwargs):
    bound = jnp.reshape(score_bound * BOUND_SLACK, (1,)).astype(F32)
    return lax.cond(bound[0] <= MAX_SCORE_BOUND,
                    lambda: _attention(*args, bound=bound, **kwargs),
                    lambda: _attention(*args, **kwargs))


def _out_ffn_kernel(x_ref, a_ref, mod_ref, g_ref, wo_ref, wg_ref, wu_ref, wd_ref, o_ref, *, chunks):
    mod = lambda r: mod_ref[0, r:r + 1, :]
    x1 = x_ref[0] + mod(2) * _dot(a_ref[0], wo_ref[...])
    h = _norm_mod(x1, g_ref[...], mod(3), mod(4)).astype(BF16)
    hidden = wg_ref.shape[1]
    step = hidden // chunks
    gate_up = lambda c: (_dot(h, wg_ref[:, c:c + step]), _dot(h, wu_ref[:, c:c + step]))
    y = None
    nxt = gate_up(0)
    for c in range(0, hidden, step):
        gate, up = nxt
        if c + step < hidden:
            nxt = gate_up(c + step)
        act = (gate * jax.nn.sigmoid(gate) * up).astype(BF16)
        part = _dot(act, wd_ref[c:c + step, :])
        y = part if y is None else y + part
    o_ref[0] = x1 + mod(5) * y


def _out_ffn(x, attn, mod, g2, w_o, w_gate, w_up, w_down, layer):
    b, l, d = x.shape
    tl = min(TOKEN_TILE, l)
    tok = lambda n: pl.BlockSpec((1, tl, n), lambda b_, i: (b_, i, 0))
    of_layer = lambda w: pl.BlockSpec((None,) + w.shape[1:], lambda b_, i: (layer, 0, 0),
                                      pipeline_mode=pl.Buffered(1))
    return pl.pallas_call(
        functools.partial(_out_ffn_kernel, chunks=11),
        out_shape=jax.ShapeDtypeStruct((b, l, d), F32),
        grid=(b, l // tl),
        in_specs=[tok(d), tok(attn.shape[2]), _mod_spec(mod), _resident((1, d)),
                  _resident(w_o.shape), of_layer(w_gate), of_layer(w_up), of_layer(w_down)],
        out_specs=tok(d),
        compiler_params=_params("parallel", "parallel"),
        name="out_ffn",
    )(x, attn, mod, g2.reshape(1, d), w_o, w_gate, w_up, w_down)


def _max_abs(a):
    return jnp.max(jnp.abs(a))


def _block_diag_ones(n, group):
    idx = np.arange(n) // group
    return jnp.asarray(idx[:, None] == idx[None, :], BF16)


def _tile_gain(g, scale=1.0):
    return jnp.tile(g * scale, MXU_COLS // g.shape[0]).reshape(1, MXU_COLS)


def _rope_tables(n_tokens, r):
    quarter = r // 4
    n_rows = n_tokens // GRID_W
    inv_freq = ROPE_THETA ** (-jnp.arange(quarter, dtype=F32) / quarter)
    ang = jnp.arange(max(n_rows, GRID_W), dtype=F32)[:, None] * inv_freq[None, :]
    by_row = lambda a: jnp.broadcast_to(a[:n_rows, None, :], (n_rows, GRID_W, quarter)).reshape(n_tokens, quarter)
    by_col = lambda a: jnp.broadcast_to(a[None, :GRID_W, :], (n_rows, GRID_W, quarter)).reshape(n_tokens, quarter)
    cos, sin = jnp.cos(ang), jnp.sin(ang)
    cos_head = jnp.concatenate([by_row(cos), by_row(cos), by_col(cos), by_col(cos)], axis=-1)
    sin_head = jnp.concatenate([-by_row(sin), by_row(sin), -by_col(sin), by_col(sin)], axis=-1)
    return jnp.tile(cos_head, (1, MXU_COLS // r)), jnp.tile(sin_head, (1, MXU_COLS // r))


_GQA_HEADS = np.array([8 * (p // 4) + 4 * e + (p % 4) for p in range(N_HEADS // 2) for e in range(2)])
_GQA_COLS = (_GQA_HEADS[:, None] * HEAD_DIM + np.arange(HEAD_DIM)[None, :]).reshape(-1)


def _na_bias_table(rpb, q_rows):
    n_heads, _, n_dc = rpb.shape
    qc = np.arange(LANES)[None, :] % GRID_W
    kc = np.arange(GRID_W)[:, None]
    start = np.clip(qc - NA_WIN_COLS // 2, 0, GRID_W - NA_WIN_COLS)
    valid = (kc >= start) & (kc < start + NA_WIN_COLS)
    dcol = np.clip(kc - qc + NA_WIN_COLS - 1, 0, n_dc - 1) + n_dc * (np.arange(LANES)[None, :] // GRID_W)
    onehot = (np.arange(2 * n_dc)[:, None] == dcol.reshape(1, -1)).astype(np.float32)
    n_e = 2 * q_rows + 2 * NA_WIN_ROWS - 2
    rp = jnp.pad(rpb, ((0, 0), (q_rows, q_rows), (0, 0)))
    pairs = jnp.concatenate([rp[:, 1:1 + n_e], rp[:, 0:n_e]], axis=-1)
    t = jnp.dot(pairs.reshape(n_heads * n_e, 2 * n_dc), onehot, precision=lax.Precision.HIGHEST)
    return jnp.where(valid[None, None], t.reshape(n_heads, n_e, GRID_W, LANES), NEG_INF)


def kernel(x_prompt, x_sample, cache_na_k, cache_na_v, cache_swa_k, cache_swa_v, cache_mla_ckv, cache_mla_kpe, cache_gqa_k, cache_gqa_v, c, c_ctx, norm1_g, norm2_g, w_mod, b_mod, na_w_qkv, na_q_g, na_k_g, na_rpb, na_w_o, swa_w_qkv, swa_q_g, swa_k_g, swa_sink, swa_w_o, mla_w_dq, mla_q_lora_g, mla_w_uq, mla_q_nope_g, mla_q_pe_g, mla_w_dkv, mla_kv_lora_g, mla_k_pe_g, mla_w_ukv, mla_k_nope_g, mla_w_o, gqa_w_qkv, gqa_q_g, gqa_k_g, gqa_w_o, ffn_w_gate, ffn_w_up, ffn_w_down):
    depth, d = norm1_g.shape
    bp, lp, _ = x_prompt.shape
    bs, ls, _ = x_sample.shape
    hq, hkv = N_HEADS * HEAD_DIM, N_KV_HEADS * HEAD_DIM
    scale = HEAD_DIM ** -0.5 * LOG2E

    cond = jnp.concatenate([c_ctx[None, :], c, jnp.zeros((16 - 1 - bs, d), F32)], axis=0)
    mods = _adaln(cond, w_mod, b_mod).reshape(depth, 16, 6, d)
    rope64 = _rope_tables(ls, HEAD_DIM)
    rope32 = _rope_tables(ls, MLA_ROPE)
    flat = lambda a, r: a[:, r].reshape(a.shape[0], a.shape[2], -1)

    wg, wu, wd = ffn_w_gate.astype(BF16), ffn_w_up.astype(BF16), ffn_w_down.astype(BF16)
    fold_n = max(1, TOKEN_TILE // lp)
    fold_n = fold_n if bp % fold_n == 0 else 1
    fold = lambda a: a.reshape(a.shape[0] // fold_n, fold_n * a.shape[1], a.shape[2])
    unfold = lambda a: lax.optimization_barrier(a.reshape(bp, lp, a.shape[2]))
    xp, xs = fold(x_prompt), x_sample
    outs = {}
    for li in range(depth):
        kind, r = li % 4, li // 4
        mp, ms = mods[li, 0:1], mods[li, 1:1 + bs]
        g1 = norm1_g[li]
        if kind in (0, 1, 3):
            w_qkv, q_g, k_g, w_o = ((na_w_qkv, na_q_g, na_k_g, na_w_o), (swa_w_qkv, swa_q_g, swa_k_g, swa_w_o),
                                    None, (gqa_w_qkv, gqa_q_g, gqa_k_g, gqa_w_o))[kind]
            w_qkv, w_o = w_qkv[r], w_o[r]
            nk = hq if kind == 0 else hkv
            if kind != 0:
                w_qkv = jnp.concatenate([w_qkv[:, :hq][:, _GQA_COLS], w_qkv[:, hq:]], axis=1)
                w_o = w_o[_GQA_COLS, :]
            w_qkv, w_o = w_qkv.astype(BF16), w_o.astype(BF16)
            qp, kp, vp = map(unfold, _qkv_project(xp, g1, mp, w_qkv, q_g[r] * scale, k_g[r], hq, nk, F32))
            qs, ks, vs = _qkv_project(xs, g1, ms, w_qkv, q_g[r] * scale, k_g[r], hq, nk, BF16,
                                      rope_tables=None if kind == 0 else rope64)
            qk_bound = HEAD_DIM * _max_abs(q_g[r] * scale) * _max_abs(k_g[r])
            if kind == 0:
                kc, vc = flat(cache_na_k, r), flat(cache_na_v, r)
                rpb = na_rpb[r] * LOG2E
                ap = _attention_guarded(qk_bound, qp, kp, vp, pp=PROMPT_PAIRS)
                as_ = _attention_guarded(qk_bound + _max_abs(rpb), qs, ks, vs, mode="na", kc=kc, vc=vc,
                                         na_table=_na_bias_table(rpb, WINDOW_Q_TILE // GRID_W),
                                         pp=WINDOW_PAIRS, q_tile=WINDOW_Q_TILE)
                outs["na_k"], outs["na_v"] = kp, vp
            elif kind == 1:
                kc, vc = flat(cache_swa_k, r), flat(cache_swa_v, r)
                sink = swa_sink[r][_GQA_HEADS] * LOG2E
                ap = _attention_guarded(qk_bound, qp, kp, vp, sink=sink, pp=PROMPT_PAIRS)
                as_ = _attention_guarded(qk_bound, qs, ks, vs, mode="swa", kc=kc, vc=vc, sink=sink,
                                         pp=WINDOW_PAIRS, q_tile=WINDOW_Q_TILE)
                outs["swa_k"], outs["swa_v"] = kp, vp
            else:
                kc, vc = flat(cache_gqa_k, r), flat(cache_gqa_v, r)
                ap = _attention_guarded(qk_bound, qp, kp, vp, pp=PROMPT_PAIRS)
                as_ = _attention_guarded(qk_bound, qs, ks, vs, kc=kc, vc=vc, pp=LATENT_PAIRS)
                outs["gqa_k"], outs["gqa_v"] = kp, vp
        else:
            dk = MLA_NOPE + MLA_ROPE
            w_uq = mla_w_uq[r].reshape(-1, N_HEADS, dk)
            w_uq = jnp.concatenate([w_uq[:, :, :MLA_NOPE].reshape(-1, N_HEADS * MLA_NOPE),
                                    w_uq[:, :, MLA_NOPE:].reshape(-1, N_HEADS * MLA_ROPE)], axis=1).astype(BF16)
            w_ukv = mla_w_ukv[r].reshape(-1, N_HEADS, MLA_NOPE + MLA_V)
            w_ukv = jnp.concatenate([w_ukv[:, :, :MLA_NOPE].reshape(-1, N_HEADS * MLA_NOPE),
                                     w_ukv[:, :, MLA_NOPE:].reshape(-1, N_HEADS * MLA_V)], axis=1).astype(BF16)
            w_dkv = jnp.pad(mla_w_dkv[r], ((0, 0), (0, 2 * LANES - MLA_KV_LORA - MLA_ROPE))).astype(BF16)
            w_dq, w_o = mla_w_dq[r].astype(BF16), mla_w_o[r].astype(BF16)
            qscale = dk ** -0.5 * LOG2E
            proj = functools.partial(
                _mla_project, w_dq=w_dq, q_lora_g=mla_q_lora_g[r], w_uq=w_uq, q_nope_g=mla_q_nope_g[r] * qscale,
                q_pe_g=mla_q_pe_g[r] * qscale, w_dkv=w_dkv, kv_lora_g=mla_kv_lora_g[r], k_pe_g=mla_k_pe_g[r])
            qn_p, qpe_p, ckv_p, kpe_p = proj(xp, g1, mp)
            qn_s, qpe_s, ckv_s, kpe_s = proj(xs, g1, ms, rope_tables=rope32)
            kn_p, v_p, kpt_p = map(unfold, _mla_expand(ckv_p, kpe_p, w_ukv, mla_k_nope_g[r]))
            qn_p, qpe_p, ckv_p, kpe_p = map(unfold, (qn_p, qpe_p, ckv_p, kpe_p))
            kn_s, v_s, kpt_s = _mla_expand(ckv_s, kpe_s, w_ukv, mla_k_nope_g[r])
            kn_c, v_c, kpt_c = _mla_expand(cache_mla_ckv[:, r], cache_mla_kpe[:, r], w_ukv, mla_k_nope_g[r])
            q_norm = jnp.sqrt(MLA_NOPE * _max_abs(mla_q_nope_g[r] * qscale) ** 2
                              + MLA_ROPE * _max_abs(mla_q_pe_g[r] * qscale) ** 2)
            k_norm = jnp.sqrt(MLA_NOPE * _max_abs(mla_k_nope_g[r]) ** 2 + MLA_ROPE * _max_abs(mla_k_pe_g[r]) ** 2)
            ap = _attention_guarded(q_norm * k_norm, qn_p, kn_p, v_p, q2=qpe_p, k2=kpt_p, pp=PROMPT_PAIRS)
            as_ = _attention_guarded(q_norm * k_norm, qn_s, kn_s, v_s, kc=kn_c, vc=v_c, q2=qpe_s, k2=kpt_s,
                                     kc2=kpt_c, pp=LATENT_PAIRS)
            outs["mla_ckv"], outs["mla_kpe"] = ckv_p, kpe_p[:, :, :MLA_ROPE]
        xp = _out_ffn(xp, fold(ap), mp, norm2_g[li], w_o, wg, wu, wd, li)
        xs = _out_ffn(xs, as_, ms, norm2_g[li], w_o, wg, wu, wd, li)

    heads = lambda a, h: a.reshape(bp, 1, lp, h, HEAD_DIM)
    return (unfold(xp), xs, heads(outs["na_k"], N_HEADS), heads(outs["na_v"], N_HEADS),
            heads(outs["swa_k"], N_KV_HEADS), heads(outs["swa_v"], N_KV_HEADS),
            outs["mla_ckv"][:, None], outs["mla_kpe"][:, None],
            heads(outs["gqa_k"], N_KV_HEADS), heads(outs["gqa_v"], N_KV_HEADS))
```

```python
import functools

import numpy as np
import jax
import jax.numpy as jnp
from jax import lax
from jax.experimental import pallas as pl
from jax.experimental.pallas import tpu as pltpu

GRID_W = 64
HEAD_DIM = 64
N_HEADS = 16
N_KV_HEADS = 4
NA_WIN_ROWS = 8
NA_WIN_COLS = 16
SWA_WINDOW = 128
MLA_KV_LORA = 128
MLA_NOPE = 64
MLA_ROPE = 32
MLA_V = 64
ROPE_THETA = 10000.0
EPS = 1e-6
NEG_INF = -1e30

LANES = 128
MXU_COLS = 256
VMEM_LIMIT = 56 * 1024 * 1024
TOKEN_TILE = 512
Q_TILE = 256
KEY_CHUNK = 256
LOG2E = 1.4426950408889634
PROMPT_PAIRS = 8
LATENT_PAIRS = 4
WINDOW_Q_TILE = 128
WINDOW_PAIRS = 8
BOUND_SLACK = 1.01
MAX_SCORE_BOUND = 48.0

F32 = jnp.float32
BF16 = jnp.bfloat16


def _params(*sem):
    return pltpu.CompilerParams(dimension_semantics=sem, vmem_limit_bytes=VMEM_LIMIT)


def _resident(shape):
    nd = len(shape)
    return pl.BlockSpec(shape, lambda *_: (0,) * nd, pipeline_mode=pl.Buffered(1))


def _dot(a, b):
    return jnp.dot(a, b, preferred_element_type=F32)


def _dot_nt(a, b):
    return lax.dot_general(a, b, (((1,), (1,)), ((), ())), preferred_element_type=F32)


def _dot_tn(a, b):
    return lax.dot_general(a, b, (((0,), (0,)), ((), ())), preferred_element_type=F32)


def _norm_mod(x, g, shift, scale):
    ms = jnp.mean(x * x, axis=-1, keepdims=True)
    y = x * lax.rsqrt(ms + EPS) * g
    return y * (1.0 + scale) + shift


def _row_rms(x, g, n):
    ms = jnp.sum(x * x, axis=-1, keepdims=True) * (1.0 / n)
    return x * lax.rsqrt(ms + EPS) * g


def _group_rms(y, gain, bd, group):
    ss = _dot((y * y).astype(BF16), bd)
    return y * lax.rsqrt(ss * (1.0 / group) + EPS) * gain


def _pipelined_blocks(project, jobs):
    y_next = project(jobs[0][0])
    for j, (_, ref, col, finish) in enumerate(jobs):
        y = y_next
        if j + 1 < len(jobs):
            y_next = project(jobs[j + 1][0])
        ref[0, :, col:col + MXU_COLS] = finish(y).astype(ref.dtype)


def _rope(y, cos, sin_signed, quarter):
    n = y.shape[-1]
    lane = lax.broadcasted_iota(jnp.int32, (1, n), 1)
    first = (lane & quarter) == 0
    partner = jnp.where(first, pltpu.roll(y, n - quarter, 1), pltpu.roll(y, quarter, 1))
    return y * cos + partner * sin_signed


def _adaln_kernel(cond_ref, w_ref, b_ref, o_ref):
    c = cond_ref[...]
    a = (c * jax.nn.sigmoid(c)).astype(BF16)
    o_ref[0] = _dot(a, w_ref[0].astype(BF16)) + b_ref[0]


def _adaln(cond, w_mod, b_mod):
    depth, d, n = w_mod.shape
    rows = cond.shape[0]
    tn = 1536
    return pl.pallas_call(
        _adaln_kernel,
        out_shape=jax.ShapeDtypeStruct((depth, rows, n), F32),
        grid=(depth, n // tn),
        in_specs=[
            pl.BlockSpec((rows, d), lambda l, j: (0, 0)),
            pl.BlockSpec((1, d, tn), lambda l, j: (l, 0, j)),
            pl.BlockSpec((1, 1, tn), lambda l, j: (l, 0, j)),
        ],
        out_specs=pl.BlockSpec((1, rows, tn), lambda l, j: (l, 0, j)),
        compiler_params=_params("parallel", "parallel"),
        name="adaln",
    )(cond, w_mod, b_mod.reshape(depth, 1, n))


def _mod_spec(mod):
    if mod.shape[0] == 1:
        return pl.BlockSpec((1,) + mod.shape[1:], lambda b, i: (0, 0, 0))
    return pl.BlockSpec((1,) + mod.shape[1:], lambda b, i: (b, 0, 0))


def _qkv_kernel(*refs, nq, nk, rope):
    x_ref, g_ref, mod_ref, w_ref, qg_ref, kg_ref, bd_ref = refs[:7]
    if rope:
        cos, sin = refs[7][...], refs[8][...]
        q_ref, k_ref, v_ref = refs[9:]
    else:
        cos = sin = None
        q_ref, k_ref, v_ref = refs[7:]
    h = _norm_mod(x_ref[0], g_ref[...], mod_ref[0, 0:1, :], mod_ref[0, 1:2, :]).astype(BF16)
    bd = bd_ref[...]

    def head(y, gain):
        y = _group_rms(y, gain, bd, HEAD_DIM)
        return _rope(y, cos, sin, HEAD_DIM // 4) if rope else y

    jobs = [(c, q_ref, c, lambda y: head(y, qg_ref[...])) for c in range(0, nq, MXU_COLS)]
    jobs += [(nq + c, k_ref, c, lambda y: head(y, kg_ref[...])) for c in range(0, nk, MXU_COLS)]
    jobs += [(nq + nk + c, v_ref, c, lambda y: y) for c in range(0, nk, MXU_COLS)]
    _pipelined_blocks(lambda col: _dot(h, w_ref[:, col:col + MXU_COLS]), jobs)


def _qkv_project(x, g, mod, w, q_gain, k_gain, nq, nk, kv_dtype, rope_tables=None):
    b, l, d = x.shape
    tl = min(TOKEN_TILE, l)
    rope = rope_tables is not None
    bd = _block_diag_ones(MXU_COLS, HEAD_DIM)
    args = [x, g.reshape(1, d), mod, w, _tile_gain(q_gain), _tile_gain(k_gain), bd]
    in_specs = [
        pl.BlockSpec((1, tl, d), lambda b_, i: (b_, i, 0)),
        _resident((1, d)),
        _mod_spec(mod),
        _resident(w.shape),
        _resident((1, MXU_COLS)),
        _resident((1, MXU_COLS)),
        _resident(bd.shape),
    ]
    if rope:
        args += list(rope_tables)
        in_specs += [pl.BlockSpec((tl, MXU_COLS), lambda b_, i: (i, 0))] * 2
    out = lambda n, dt: (jax.ShapeDtypeStruct((b, l, n), dt),
                         pl.BlockSpec((1, tl, n), lambda b_, i: (b_, i, 0)))
    shapes, specs = zip(out(nq, BF16), out(nk, kv_dtype), out(nk, kv_dtype))
    return pl.pallas_call(
        functools.partial(_qkv_kernel, nq=nq, nk=nk, rope=rope),
        out_shape=shapes,
        grid=(b, l // tl),
        in_specs=in_specs,
        out_specs=specs,
        compiler_params=_params("parallel", "parallel"),
        name="qkv_project",
    )(*args)


def _mla_project_kernel(*refs, rope):
    (x_ref, g_ref, mod_ref, wdq_ref, qlg_ref, wuq_ref, qng_ref, qpg_ref, bdn_ref, bdp_ref,
     wdkv_ref, kvg_ref, kpg_ref) = refs[:13]
    if rope:
        cos, sin = refs[13][...], refs[14][...]
        qn_ref, qp_ref, ckv_ref, kpe_ref = refs[15:]
    else:
        cos = sin = None
        qn_ref, qp_ref, ckv_ref, kpe_ref = refs[13:]
    h = _norm_mod(x_ref[0], g_ref[...], mod_ref[0, 0:1, :], mod_ref[0, 1:2, :]).astype(BF16)
    q_lora = wdq_ref.shape[1]
    ql = _row_rms(_dot(h, wdq_ref[...]), qlg_ref[...], q_lora).astype(BF16)
    n_nope = qn_ref.shape[2]
    n_pe = qp_ref.shape[2]
    def pe_head(y):
        y = _group_rms(y, qpg_ref[...], bdp_ref[...], MLA_ROPE)
        return _rope(y, cos, sin, MLA_ROPE // 4) if rope else y

    jobs = [(c, qn_ref, c, lambda y: _group_rms(y, qng_ref[...], bdn_ref[...], MLA_NOPE))
            for c in range(0, n_nope, MXU_COLS)]
    jobs += [(n_nope + c, qp_ref, c, pe_head) for c in range(0, n_pe, MXU_COLS)]
    _pipelined_blocks(lambda col: _dot(ql, wuq_ref[:, col:col + MXU_COLS]), jobs)
    ckv_kpe = _dot(h, wdkv_ref[...])
    ckv_ref[0] = _row_rms(ckv_kpe[:, :MLA_KV_LORA], kvg_ref[...], MLA_KV_LORA)
    kpe = _row_rms(ckv_kpe[:, MLA_KV_LORA:], kpg_ref[...], MLA_ROPE)
    if rope:
        kpe = _rope(kpe, cos[:, :LANES], sin[:, :LANES], MLA_ROPE // 4)
    kpe_ref[0] = kpe


def _mla_project(x, g, mod, w_dq, q_lora_g, w_uq, q_nope_g, q_pe_g, w_dkv, kv_lora_g, k_pe_g,
                 rope_tables=None):
    b, l, d = x.shape
    tl = min(TOKEN_TILE, l)
    rope = rope_tables is not None
    n_nope, n_pe = N_HEADS * MLA_NOPE, N_HEADS * MLA_ROPE
    bdn = _block_diag_ones(MXU_COLS, MLA_NOPE)
    bdp = _block_diag_ones(MXU_COLS, MLA_ROPE)
    kpg = jnp.pad(k_pe_g, (0, LANES - MLA_ROPE)).reshape(1, LANES)
    args = [x, g.reshape(1, d), mod, w_dq, q_lora_g.reshape(1, -1), w_uq, _tile_gain(q_nope_g),
            _tile_gain(q_pe_g), bdn, bdp, w_dkv, kv_lora_g.reshape(1, -1), kpg]
    in_specs = [pl.BlockSpec((1, tl, d), lambda b_, i: (b_, i, 0)), _resident((1, d)), _mod_spec(mod)]
    in_specs += [_resident(a.shape) for a in args[3:]]
    if rope:
        args += list(rope_tables)
        in_specs += [pl.BlockSpec((tl, MXU_COLS), lambda b_, i: (i, 0))] * 2
    out = lambda n, dt: (jax.ShapeDtypeStruct((b, l, n), dt),
                         pl.BlockSpec((1, tl, n), lambda b_, i: (b_, i, 0)))
    shapes, specs = zip(out(n_nope, BF16), out(n_pe, BF16), out(MLA_KV_LORA, F32), out(LANES, F32))
    return pl.pallas_call(
        functools.partial(_mla_project_kernel, rope=rope),
        out_shape=shapes,
        grid=(b, l // tl),
        in_specs=in_specs,
        out_specs=specs,
        compiler_params=_params("parallel", "parallel"),
        name="mla_project",
    )(*args)


def _mla_expand_kernel(ckv_ref, kpe_ref, w_ref, kng_ref, bd_ref, tile_ref, kn_ref, v_ref, kpt_ref):
    c = ckv_ref[0].astype(BF16)
    n = kn_ref.shape[2]
    jobs = [(col, kn_ref, col, lambda y: _group_rms(y, kng_ref[...], bd_ref[...], MLA_NOPE))
            for col in range(0, n, MXU_COLS)]
    jobs += [(n + col, v_ref, col, lambda y: y) for col in range(0, n, MXU_COLS)]
    _pipelined_blocks(lambda col: _dot(c, w_ref[:, col:col + MXU_COLS]), jobs)
    kpt_ref[0] = _dot(kpe_ref[0].astype(BF16), tile_ref[...]).astype(kpt_ref.dtype)


def _mla_expand(ckv, kpe, w_ukv, k_nope_g):
    b, l, _ = ckv.shape
    tl = min(TOKEN_TILE, l)
    n = N_HEADS * MLA_NOPE
    pw = kpe.shape[2]
    bd = _block_diag_ones(MXU_COLS, MLA_NOPE)
    tile = np.zeros((pw, LANES), np.float32)
    for r in range(LANES // MLA_ROPE):
        tile[np.arange(MLA_ROPE), r * MLA_ROPE + np.arange(MLA_ROPE)] = 1.0
    tile = jnp.asarray(tile, BF16)
    out = lambda m: (jax.ShapeDtypeStruct((b, l, m), BF16), pl.BlockSpec((1, tl, m), lambda b_, i: (b_, i, 0)))
    shapes, specs = zip(out(n), out(n), out(LANES))
    return pl.pallas_call(
        _mla_expand_kernel,
        out_shape=shapes,
        grid=(b, l // tl),
        in_specs=[
            pl.BlockSpec((1, tl, MLA_KV_LORA), lambda b_, i: (b_, i, 0)),
            pl.BlockSpec((1, tl, pw), lambda b_, i: (b_, i, 0)),
            _resident(w_ukv.shape),
            _resident((1, MXU_COLS)),
            _resident(bd.shape),
            _resident(tile.shape),
        ],
        out_specs=specs,
        compiler_params=_params("parallel", "parallel"),
        name="mla_expand",
    )(ckv, kpe, w_ukv, _tile_gain(k_nope_g), bd, tile)


def _key_norm_kernel(k_ref, o_ref, *, group):
    x = k_ref[0].astype(F32)
    sq = x * x
    width = min(LANES, sq.shape[1])
    lane = lax.broadcasted_iota(jnp.int32, (1, width), 1)
    best = None
    for c in range(0, sq.shape[1], width):
        tile = sq[:, c:c + width]
        for g0 in range(0, width, group):
            part = tile if group == width else jnp.where((lane >= g0) & (lane < g0 + group), tile, 0.0)
            top = jnp.max(jnp.sum(part, axis=-1, keepdims=True), axis=0, keepdims=True)
            best = top if best is None else jnp.maximum(best, top)
    o_ref[0] = jnp.broadcast_to(best, o_ref.shape[1:])


def _key_norm_max(k, group):
    b, l, n = k.shape
    out = pl.pallas_call(
        functools.partial(_key_norm_kernel, group=group),
        out_shape=jax.ShapeDtypeStruct((b, 8, LANES), F32),
        grid=(b,),
        in_specs=[pl.BlockSpec((1, l, n), lambda b_: (b_, 0, 0))],
        out_specs=pl.BlockSpec((1, 8, LANES), lambda b_: (b_, 0, 0)),
        compiler_params=_params("parallel"),
        name="key_norm_max",
    )(k)
    return jnp.sqrt(jnp.max(out))


def _attn_kernel(*refs, mode, ctx, mla, sink, tq, rows, pp, group, bounded, ctx_bounded):
    it = iter(refs)
    q_ref, k_ref, v_ref = next(it), next(it), next(it)
    kc_ref, vc_ref = (next(it), next(it)) if ctx else (None, None)
    q2_ref, k2_ref = (next(it), next(it)) if mla else (None, None)
    kc2_ref = next(it) if (mla and ctx) else None
    sink_ref = next(it) if sink else None
    tbl_ref = next(it) if mode in ("na", "swa") else None
    bound_ref = next(it) if bounded else None
    o_ref = next(it)

    g = pl.program_id(1)
    t = pl.program_id(2)
    lane = lax.broadcasted_iota(jnp.int32, (1, LANES), 1)
    plans = [_attn_pair_plan(i, g, t, lane, q_ref, k_ref, v_ref, kc_ref, vc_ref, q2_ref, k2_ref, kc2_ref,
                             sink_ref, tbl_ref, mode=mode, ctx=ctx, mla=mla, sink=sink, tq=tq, rows=rows,
                             pp=pp, group=group, ctx_bounded=ctx_bounded)
             for i in range(pp)]
    floor = bound_ref[0] if bounded else None

    def scores(i, j):
        kk, _, fix, _ = plans[i][1][j]
        s = _dot(kk, plans[i][0])
        return s if fix is None else fix(s)

    n_chunks = len(plans[0][1])
    state = [None] * pp
    s_next = [scores(i, 0) for i in range(pp)]
    for j in range(n_chunks):
        for i in range(pp):
            s = s_next[i]
            if j + 1 < n_chunks:
                s_next[i] = scores(i, j + 1)
            _, vv, _, exact_max = plans[i][1][j]
            state[i] = _softmax_step(state[i], s, vv, plans[i][2], floor, exact_max or not bounded)
    for i in range(pp):
        m, den, acc = state[i]
        if sink:
            den = den + jnp.exp2(plans[i][2] - m)
        o = acc / den
        o_pair = jnp.concatenate([o[:HEAD_DIM, :tq], o[HEAD_DIM:, tq:]], axis=0)
        o_ref[0, :, i * LANES:(i + 1) * LANES] = o_pair.T.astype(o_ref.dtype)


def _softmax_step(state, s, vv, sk, floor, exact_max):
    def probs(m):
        pr = jnp.exp2(s - m)
        return pr.astype(BF16), jnp.sum(pr, axis=0, keepdims=True)

    if state is None:
        m = None
        if floor is not None:
            m = jnp.full((1, s.shape[1]), floor, F32)
        if sk is not None:
            m = sk if m is None else jnp.maximum(m, sk)
        if exact_max:
            m_c = jnp.max(s, axis=0, keepdims=True)
            m = m_c if m is None else jnp.maximum(m, m_c)
        pr, total = probs(m)
        return m, total, _dot_tn(vv, pr)
    m, den, acc = state
    if exact_max:
        m_new = jnp.maximum(m, jnp.max(s, axis=0, keepdims=True))
        alpha = jnp.exp2(m - m_new)
        pr, total = probs(m_new)
        return m_new, alpha * den + total, alpha * acc + _dot_tn(vv, pr)
    pr, total = probs(m)
    return m, den + total, acc + _dot_tn(vv, pr)


def _attn_pair_plan(i, g, t, lane, q_ref, k_ref, v_ref, kc_ref, vc_ref, q2_ref, k2_ref, kc2_ref, sink_ref,
                    tbl_ref, *, mode, ctx, mla, sink, tq, rows, pp, group, ctx_bounded):
    ql = slice(i * LANES, (i + 1) * LANES)
    kvi = i // group if pp >= group else 0
    kl = slice(kvi * LANES, (kvi + 1) * LANES)

    row = lax.broadcasted_iota(jnp.int32, (LANES, 1), 0)

    def stack(x, m0, m1):
        xt = x.astype(F32).T
        return jnp.concatenate([jnp.where(m0, xt, 0.0), jnp.where(m1, xt, 0.0)], axis=1).astype(BF16)

    qs = stack(q_ref[0, :, ql], row < HEAD_DIM, row >= HEAD_DIM)
    if mla:
        if pp == 1:
            q2, base = q2_ref[0], (g % 2) * (2 * MLA_ROPE)
        else:
            q2, base = q2_ref[0, :, (i // 2) * LANES:(i // 2 + 1) * LANES], (i % 2) * (2 * MLA_ROPE)
        m0 = (row >= base) & (row < base + MLA_ROPE)
        m1 = (row >= base + MLA_ROPE) & (row < base + 2 * MLA_ROPE)
        qs = jnp.concatenate([qs, stack(q2, m0, m1)], axis=0)

    def keys(ref, ref2, rows_):
        kk = ref[0, rows_, kl].astype(BF16)
        if mla:
            kk = jnp.concatenate([kk, ref2[0, rows_, :].astype(BF16)], axis=1)
        return kk

    chunks = []
    if ctx:
        lc = kc_ref.shape[1]
        kc_len = min(KEY_CHUNK, lc)
        for c in range(0, lc, kc_len):
            rows_ = slice(c, c + kc_len)
            chunks.append((keys(kc_ref, kc2_ref, rows_), vc_ref[0, rows_, kl].astype(BF16), None,
                           not ctx_bounded))
    if mode == "full":
        lk = k_ref.shape[1]
        kc_len = min(KEY_CHUNK, lk)
        for c in range(0, lk, kc_len):
            rows_ = slice(c, c + kc_len)
            chunks.append((keys(k_ref, k2_ref, rows_), v_ref[0, rows_, kl].astype(BF16), None, False))
    elif mode == "swa":
        win = tq + 2 * SWA_WINDOW
        ws = pl.multiple_of(jnp.clip(t * tq - SWA_WINDOW, 0, k_ref.shape[1] - win), SWA_WINDOW)

        def band(s):
            return s + tbl_ref[(t * tq - ws) // SWA_WINDOW]

        chunks.append((k_ref[0, pl.ds(ws, win), kl], v_ref[0, pl.ds(ws, win), kl], band, False))
    else:
        qrows = tq // GRID_W
        wrows = qrows + NA_WIN_ROWS
        r0 = t * qrows
        ws_row = jnp.clip(r0 - NA_WIN_ROWS // 2, 0, rows - wrows)
        ws = pl.multiple_of(ws_row * GRID_W, GRID_W)

        def window_bias(s):
            e0 = ws_row - r0 + (NA_WIN_ROWS - 1) + qrows - 1
            out_rows = []
            for j in range(wrows):
                kr = ws_row + j
                blocks = []
                for hh in range(2):
                    for ii in range(qrows // 2):
                        ok = []
                        for qr in (r0 + 2 * ii, r0 + 2 * ii + 1):
                            rs = jnp.clip(qr - NA_WIN_ROWS // 2, 0, rows - NA_WIN_ROWS)
                            ok.append(((kr >= rs) & (kr < rs + NA_WIN_ROWS)).astype(jnp.int32))
                        valid = jnp.where(lane < GRID_W, ok[0], ok[1]) > 0
                        lo = hh * tq + ii * LANES
                        blk = (s[j * GRID_W:(j + 1) * GRID_W, lo:lo + LANES]
                               + tbl_ref[2 * i + hh, e0 + j - 2 * ii])
                        blocks.append(jnp.where(valid, blk, NEG_INF))
                out_rows.append(jnp.concatenate(blocks, axis=1))
            return jnp.concatenate(out_rows, axis=0)

        chunks.append((k_ref[0, pl.ds(ws, wrows * GRID_W), kl], v_ref[0, pl.ds(ws, wrows * GRID_W), kl],
                       window_bias, False))

    sk = None
    if sink:
        first = lax.broadcasted_iota(jnp.int32, (1, 2 * tq), 1) < tq
        head = 2 * (g * pp + i)
        sk = jnp.where(first, sink_ref[head], sink_ref[head + 1])
    return qs, chunks, sk


def _attention(q, k, v, *, mode="full", kc=None, vc=None, q2=None, k2=None, kc2=None, sink=None,
               na_table=None, bound=None, ctx_bounded=False, pp=1, q_tile=Q_TILE):
    b, lq, nq = q.shape
    lk = k.shape[1]
    pairs = nq // LANES
    group = pairs // (k.shape[2] // LANES)
    tq = min(q_tile, lq)
    ctx, mla = kc is not None, q2 is not None
    assert pairs % pp == 0 and (pp % group == 0 or group % pp == 0)
    assert not mla or pp == 1 or pp % 2 == 0

    kv_lanes = LANES * max(1, pp // group)
    qmap = lambda b_, g, t: (b_, t, g)
    if pp >= group:
        kvmap = lambda b_, g, t: (b_, 0, g)
    else:
        kvmap = lambda b_, g, t: (b_, 0, (g * pp) // group)
    shared = lambda b_, g, t: (b_, 0, 0)
    args = [q, k, v]
    in_specs = [pl.BlockSpec((1, tq, pp * LANES), qmap),
                pl.BlockSpec((1, lk, kv_lanes), kvmap),
                pl.BlockSpec((1, lk, kv_lanes), kvmap)]
    if ctx:
        lc = kc.shape[1]
        args += [kc, vc]
        in_specs += [pl.BlockSpec((1, lc, kv_lanes), kvmap)] * 2
    if mla:
        args += [q2, k2]
        if pp == 1:
            in_specs += [pl.BlockSpec((1, tq, LANES), lambda b_, g, t: (b_, t, g // 2))]
        else:
            in_specs += [pl.BlockSpec((1, tq, pp * LANES // 2), qmap)]
        in_specs += [pl.BlockSpec((1, lk, LANES), shared)]
        if ctx:
            args += [kc2]
            in_specs += [pl.BlockSpec((1, kc2.shape[1], LANES), shared)]
    if sink is not None:
        args += [sink]
        in_specs += [pl.BlockSpec(memory_space=pltpu.SMEM)]
    if mode == "na":
        args += [na_table]
        in_specs += [pl.BlockSpec((2 * pp,) + na_table.shape[1:], lambda b_, g, t: (g, 0, 0, 0))]
    if mode == "swa":
        assert tq % SWA_WINDOW == 0
        mask = _swa_band_masks(tq)
        args += [mask]
        in_specs += [_resident(mask.shape)]
    if bound is not None:
        args += [bound]
        in_specs += [pl.BlockSpec(memory_space=pltpu.SMEM)]
    return pl.pallas_call(
        functools.partial(_attn_kernel, mode=mode, ctx=ctx, mla=mla, sink=sink is not None, tq=tq,
                          rows=lq // GRID_W, pp=pp, group=group, bounded=bound is not None,
                          ctx_bounded=ctx_bounded and bound is not None),
        out_shape=jax.ShapeDtypeStruct((b, lq, nq), BF16),
        grid=(b, pairs // pp, lq // tq),
        in_specs=in_specs,
        out_specs=pl.BlockSpec((1, tq, pp * LANES), qmap),
        compiler_params=_params("parallel", "parallel", "parallel"),
        name="attention_" + mode,
    )(*args)


def _attention_guarded(score_bound, *args, **kwargs):
    bound = jnp.reshape(score_bound * BOUND_SLACK, (1,)).astype(F32)
    return lax.cond(bound[0] <= MAX_SCORE_BOUND,
                    lambda: _attention(*args, bound=bound, **kwargs),
                    lambda: _attention(*args, **kwargs))


def _out_ffn_kernel(x_ref, a_ref, mod_ref, g_ref, wo_ref, wg_ref, wu_ref, wd_ref, o_ref, *, chunks):
    mod = lambda r: mod_ref[0, r:r + 1, :]
    x1 = x_ref[0] + mod(2) * _dot(a_ref[0], wo_ref[...])
    h = _norm_mod(x1, g_ref[...], mod(3), mod(4)).astype(BF16)
    hidden = wg_ref.shape[1]
    step = hidden // chunks
    gate_up = lambda c: (_dot(h, wg_ref[:, c:c + step]), _dot(h, wu_ref[:, c:c + step]))
    y = None
    nxt = gate_up(0)
    for c in range(0, hidden, step):
        gate, up = nxt
        if c + step < hidden:
            nxt = gate_up(c + step)
        act = (gate * jax.nn.sigmoid(gate) * up).astype(BF16)
        part = _dot(act, wd_ref[c:c + step, :])
        y = part if y is None else y + part
    o_ref[0] = x1 + mod(5) * y


def _out_ffn(x, attn, mod, g2, w_o, w_gate, w_up, w_down, layer):
    b, l, d = x.shape
    tl = min(TOKEN_TILE, l)
    tok = lambda n: pl.BlockSpec((1, tl, n), lambda b_, i: (b_, i, 0))
    of_layer = lambda w: pl.BlockSpec((None,) + w.shape[1:], lambda b_, i: (layer, 0, 0),
                                      pipeline_mode=pl.Buffered(1))
    return pl.pallas_call(
        functools.partial(_out_ffn_kernel, chunks=11),
        out_shape=jax.ShapeDtypeStruct((b, l, d), F32),
        grid=(b, l // tl),
        in_specs=[tok(d), tok(attn.shape[2]), _mod_spec(mod), _resident((1, d)),
                  _resident(w_o.shape), of_layer(w_gate), of_layer(w_up), of_layer(w_down)],
        out_specs=tok(d),
        compiler_params=_params("parallel", "parallel"),
        name="out_ffn",
    )(x, attn, mod, g2.reshape(1, d), w_o, w_gate, w_up, w_down)


def _swa_band_masks(tq):
    i = np.arange(tq + 2 * SWA_WINDOW)[None, :, None]
    j = (np.arange(2 * tq) % tq)[None, None, :]
    delta = (np.arange(3) * SWA_WINDOW)[:, None, None]
    return jnp.asarray(np.where(np.abs(i - delta - j) <= SWA_WINDOW, 0.0, NEG_INF), F32)


def _max_abs(a):
    return jnp.max(jnp.abs(a))


def _block_diag_ones(n, group):
    idx = np.arange(n) // group
    return jnp.asarray(idx[:, None] == idx[None, :], BF16)


def _tile_gain(g, scale=1.0):
    return jnp.tile(g * scale, MXU_COLS // g.shape[0]).reshape(1, MXU_COLS)


def _rope_tables(n_tokens, r):
    quarter = r // 4
    n_rows = n_tokens // GRID_W
    inv_freq = ROPE_THETA ** (-jnp.arange(quarter, dtype=F32) / quarter)
    ang = jnp.arange(max(n_rows, GRID_W), dtype=F32)[:, None] * inv_freq[None, :]
    by_row = lambda a: jnp.broadcast_to(a[:n_rows, None, :], (n_rows, GRID_W, quarter)).reshape(n_tokens, quarter)
    by_col = lambda a: jnp.broadcast_to(a[None, :GRID_W, :], (n_rows, GRID_W, quarter)).reshape(n_tokens, quarter)
    cos, sin = jnp.cos(ang), jnp.sin(ang)
    cos_head = jnp.concatenate([by_row(cos), by_row(cos), by_col(cos), by_col(cos)], axis=-1)
    sin_head = jnp.concatenate([-by_row(sin), by_row(sin), -by_col(sin), by_col(sin)], axis=-1)
    return jnp.tile(cos_head, (1, MXU_COLS // r)), jnp.tile(sin_head, (1, MXU_COLS // r))


_GQA_HEADS = np.array([8 * (p // 4) + 4 * e + (p % 4) for p in range(N_HEADS // 2) for e in range(2)])
_GQA_COLS = (_GQA_HEADS[:, None] * HEAD_DIM + np.arange(HEAD_DIM)[None, :]).reshape(-1)


def _na_bias_table(rpb, q_rows):
    n_heads, _, n_dc = rpb.shape
    qc = np.arange(LANES)[None, :] % GRID_W
    kc = np.arange(GRID_W)[:, None]
    start = np.clip(qc - NA_WIN_COLS // 2, 0, GRID_W - NA_WIN_COLS)
    valid = (kc >= start) & (kc < start + NA_WIN_COLS)
    dcol = np.clip(kc - qc + NA_WIN_COLS - 1, 0, n_dc - 1) + n_dc * (np.arange(LANES)[None, :] // GRID_W)
    onehot = (np.arange(2 * n_dc)[:, None] == dcol.reshape(1, -1)).astype(np.float32)
    n_e = 2 * q_rows + 2 * NA_WIN_ROWS - 2
    rp = jnp.pad(rpb, ((0, 0), (q_rows, q_rows), (0, 0)))
    pairs = jnp.concatenate([rp[:, 1:1 + n_e], rp[:, 0:n_e]], axis=-1)
    t = jnp.dot(pairs.reshape(n_heads * n_e, 2 * n_dc), onehot, precision=lax.Precision.HIGHEST)
    return jnp.where(valid[None, None], t.reshape(n_heads, n_e, GRID_W, LANES), NEG_INF)


def kernel(x_prompt, x_sample, cache_na_k, cache_na_v, cache_swa_k, cache_swa_v, cache_mla_ckv, cache_mla_kpe, cache_gqa_k, cache_gqa_v, c, c_ctx, norm1_g, norm2_g, w_mod, b_mod, na_w_qkv, na_q_g, na_k_g, na_rpb, na_w_o, swa_w_qkv, swa_q_g, swa_k_g, swa_sink, swa_w_o, mla_w_dq, mla_q_lora_g, mla_w_uq, mla_q_nope_g, mla_q_pe_g, mla_w_dkv, mla_kv_lora_g, mla_k_pe_g, mla_w_ukv, mla_k_nope_g, mla_w_o, gqa_w_qkv, gqa_q_g, gqa_k_g, gqa_w_o, ffn_w_gate, ffn_w_up, ffn_w_down):
    depth, d = norm1_g.shape
    bp, lp, _ = x_prompt.shape
    bs, ls, _ = x_sample.shape
    hq, hkv = N_HEADS * HEAD_DIM, N_KV_HEADS * HEAD_DIM
    scale = HEAD_DIM ** -0.5 * LOG2E

    cond = jnp.concatenate([c_ctx[None, :], c, jnp.zeros((16 - 1 - bs, d), F32)], axis=0)
    mods = _adaln(cond, w_mod, b_mod).reshape(depth, 16, 6, d)
    rope64 = _rope_tables(ls, HEAD_DIM)
    rope32 = _rope_tables(ls, MLA_ROPE)
    flat = lambda a, r: a[:, r].reshape(a.shape[0], a.shape[2], -1)

    wg, wu, wd = ffn_w_gate.astype(BF16), ffn_w_up.astype(BF16), ffn_w_down.astype(BF16)
    fold_n = max(1, TOKEN_TILE // lp)
    fold_n = fold_n if bp % fold_n == 0 else 1
    fold = lambda a: a.reshape(a.shape[0] // fold_n, fold_n * a.shape[1], a.shape[2])
    unfold = lambda a: lax.optimization_barrier(a.reshape(bp, lp, a.shape[2]))
    xp, xs = fold(x_prompt), x_sample
    outs = {}
    for li in range(depth):
        kind, r = li % 4, li // 4
        mp, ms = mods[li, 0:1], mods[li, 1:1 + bs]
        g1 = norm1_g[li]
        if kind in (0, 1, 3):
            w_qkv, q_g, k_g, w_o = ((na_w_qkv, na_q_g, na_k_g, na_w_o), (swa_w_qkv, swa_q_g, swa_k_g, swa_w_o),
                                    None, (gqa_w_qkv, gqa_q_g, gqa_k_g, gqa_w_o))[kind]
            w_qkv, w_o = w_qkv[r], w_o[r]
            nk = hq if kind == 0 else hkv
            if kind != 0:
                w_qkv = jnp.concatenate([w_qkv[:, :hq][:, _GQA_COLS], w_qkv[:, hq:]], axis=1)
                w_o = w_o[_GQA_COLS, :]
            w_qkv, w_o = w_qkv.astype(BF16), w_o.astype(BF16)
            qp, kp, vp = map(unfold, _qkv_project(xp, g1, mp, w_qkv, q_g[r] * scale, k_g[r], hq, nk, F32))
            qs, ks, vs = _qkv_project(xs, g1, ms, w_qkv, q_g[r] * scale, k_g[r], hq, nk, BF16,
                                      rope_tables=None if kind == 0 else rope64)
            kc, vc = [flat(a, r) for a in ((cache_na_k, cache_na_v), (cache_swa_k, cache_swa_v), None,
                                           (cache_gqa_k, cache_gqa_v))[kind]]
            q_norm = HEAD_DIM ** 0.5 * _max_abs(q_g[r] * scale)
            qk_bound = q_norm * HEAD_DIM ** 0.5 * _max_abs(k_g[r])
            ctx_bound = q_norm * _key_norm_max(kc, HEAD_DIM)
            if kind == 0:
                rpb = na_rpb[r] * LOG2E
                ap = _attention_guarded(qk_bound, qp, kp, vp, pp=PROMPT_PAIRS)
                as_ = _attention_guarded(jnp.maximum(qk_bound + _max_abs(rpb), ctx_bound), qs, ks, vs, mode="na",
                                         kc=kc, vc=vc, na_table=_na_bias_table(rpb, WINDOW_Q_TILE // GRID_W),
                                         ctx_bounded=True, pp=WINDOW_PAIRS, q_tile=WINDOW_Q_TILE)
                outs["na_k"], outs["na_v"] = kp, vp
            elif kind == 1:
                sink = swa_sink[r][_GQA_HEADS] * LOG2E
                ap = _attention_guarded(qk_bound, qp, kp, vp, sink=sink, pp=PROMPT_PAIRS)
                as_ = _attention_guarded(jnp.maximum(qk_bound, ctx_bound), qs, ks, vs, mode="swa", kc=kc, vc=vc,
                                         sink=sink, ctx_bounded=True, pp=WINDOW_PAIRS, q_tile=WINDOW_Q_TILE)
                outs["swa_k"], outs["swa_v"] = kp, vp
            else:
                ap = _attention_guarded(qk_bound, qp, kp, vp, pp=PROMPT_PAIRS)
                as_ = _attention_guarded(jnp.maximum(qk_bound, ctx_bound), qs, ks, vs, kc=kc, vc=vc,
                                         ctx_bounded=True, pp=LATENT_PAIRS)
                outs["gqa_k"], outs["gqa_v"] = kp, vp
        else:
            dk = MLA_NOPE + MLA_ROPE
            w_uq = mla_w_uq[r].reshape(-1, N_HEADS, dk)
            w_uq = jnp.concatenate([w_uq[:, :, :MLA_NOPE].reshape(-1, N_HEADS * MLA_NOPE),
                                    w_uq[:, :, MLA_NOPE:].reshape(-1, N_HEADS * MLA_ROPE)], axis=1).astype(BF16)
            w_ukv = mla_w_ukv[r].reshape(-1, N_HEADS, MLA_NOPE + MLA_V)
            w_ukv = jnp.concatenate([w_ukv[:, :, :MLA_NOPE].reshape(-1, N_HEADS * MLA_NOPE),
                                     w_ukv[:, :, MLA_NOPE:].reshape(-1, N_HEADS * MLA_V)], axis=1).astype(BF16)
            w_dkv = jnp.pad(mla_w_dkv[r], ((0, 0), (0, 2 * LANES - MLA_KV_LORA - MLA_ROPE))).astype(BF16)
            w_dq, w_o = mla_w_dq[r].astype(BF16), mla_w_o[r].astype(BF16)
            qscale = dk ** -0.5 * LOG2E
            proj = functools.partial(
                _mla_project, w_dq=w_dq, q_lora_g=mla_q_lora_g[r], w_uq=w_uq, q_nope_g=mla_q_nope_g[r] * qscale,
                q_pe_g=mla_q_pe_g[r] * qscale, w_dkv=w_dkv, kv_lora_g=mla_kv_lora_g[r], k_pe_g=mla_k_pe_g[r])
            qn_p, qpe_p, ckv_p, kpe_p = proj(xp, g1, mp)
            qn_s, qpe_s, ckv_s, kpe_s = proj(xs, g1, ms, rope_tables=rope32)
            kn_p, v_p, kpt_p = map(unfold, _mla_expand(ckv_p, kpe_p, w_ukv, mla_k_nope_g[r]))
            qn_p, qpe_p, ckv_p, kpe_p = map(unfold, (qn_p, qpe_p, ckv_p, kpe_p))
            kn_s, v_s, kpt_s = _mla_expand(ckv_s, kpe_s, w_ukv, mla_k_nope_g[r])
            kn_c, v_c, kpt_c = _mla_expand(cache_mla_ckv[:, r], cache_mla_kpe[:, r], w_ukv, mla_k_nope_g[r])
            q_norm = jnp.sqrt(MLA_NOPE * _max_abs(mla_q_nope_g[r] * qscale) ** 2
                              + MLA_ROPE * _max_abs(mla_q_pe_g[r] * qscale) ** 2)
            k_norm = jnp.sqrt(MLA_NOPE * _max_abs(mla_k_nope_g[r]) ** 2 + MLA_ROPE * _max_abs(mla_k_pe_g[r]) ** 2)
            ctx_k_norm = jnp.sqrt(MLA_NOPE * _max_abs(mla_k_nope_g[r]) ** 2
                                  + _key_norm_max(cache_mla_kpe[:, r], MLA_ROPE) ** 2)
            ap = _attention_guarded(q_norm * k_norm, qn_p, kn_p, v_p, q2=qpe_p, k2=kpt_p, pp=PROMPT_PAIRS)
            as_ = _attention_guarded(q_norm * jnp.maximum(k_norm, ctx_k_norm), qn_s, kn_s, v_s, kc=kn_c, vc=v_c,
                                     q2=qpe_s, k2=kpt_s, kc2=kpt_c, ctx_bounded=True, pp=LATENT_PAIRS)
            outs["mla_ckv"], outs["mla_kpe"] = ckv_p, kpe_p[:, :, :MLA_ROPE]
        xp = _out_ffn(xp, fold(ap), mp, norm2_g[li], w_o, wg, wu, wd, li)
        xs = _out_ffn(xs, as_, ms, norm2_g[li], w_o, wg, wu, wd, li)

    heads = lambda a, h: a.reshape(bp, 1, lp, h, HEAD_DIM)
    return (unfold(xp), xs, heads(outs["na_k"], N_HEADS), heads(outs["na_v"], N_HEADS),
            heads(outs["swa_k"], N_KV_HEADS), heads(outs["swa_v"], N_KV_HEADS),
            outs["mla_ckv"][:, None], outs["mla_kpe"][:, None],
            heads(outs["gqa_k"], N_KV_HEADS), heads(outs["gqa_v"], N_KV_HEADS))
```

```python
import functools

import numpy as np
import jax
import jax.numpy as jnp
from jax import lax
from jax.experimental import pallas as pl
from jax.experimental.pallas import tpu as pltpu

GRID_W = 64
HEAD_DIM = 64
N_HEADS = 16
N_KV_HEADS = 4
NA_WIN_ROWS = 8
NA_WIN_COLS = 16
SWA_WINDOW = 128
MLA_KV_LORA = 128
MLA_NOPE = 64
MLA_ROPE = 32
MLA_V = 64
ROPE_THETA = 10000.0
EPS = 1e-6
NEG_INF = -1e30

LANES = 128
MXU_COLS = 256
VMEM_LIMIT = 56 * 1024 * 1024
TOKEN_TILE = 512
Q_TILE = 256
KEY_CHUNK = 256
LOG2E = 1.4426950408889634
PROMPT_PAIRS = 8
LATENT_PAIRS = 4
WINDOW_Q_TILE = 128
WINDOW_PAIRS = 8
BOUND_SLACK = 1.01
MAX_SCORE_BOUND = 48.0

F32 = jnp.float32
BF16 = jnp.bfloat16


def _params(*sem):
    return pltpu.CompilerParams(dimension_semantics=sem, vmem_limit_bytes=VMEM_LIMIT)


def _resident(shape):
    nd = len(shape)
    return pl.BlockSpec(shape, lambda *_: (0,) * nd, pipeline_mode=pl.Buffered(1))


def _dot(a, b):
    return jnp.dot(a, b, preferred_element_type=F32)


def _dot_nt(a, b):
    return lax.dot_general(a, b, (((1,), (1,)), ((), ())), preferred_element_type=F32)


def _dot_tn(a, b):
    return lax.dot_general(a, b, (((0,), (0,)), ((), ())), preferred_element_type=F32)


def _norm_mod(x, g, shift, scale):
    ms = jnp.mean(x * x, axis=-1, keepdims=True)
    y = x * lax.rsqrt(ms + EPS) * g
    return y * (1.0 + scale) + shift


def _row_rms(x, g, n):
    ms = jnp.sum(x * x, axis=-1, keepdims=True) * (1.0 / n)
    return x * lax.rsqrt(ms + EPS) * g


def _group_rms(y, gain, bd, group):
    ss = _dot((y * y).astype(BF16), bd)
    return y * lax.rsqrt(ss * (1.0 / group) + EPS) * gain


def _pipelined_blocks(project, jobs):
    y_next = project(jobs[0][0])
    for j, (_, ref, col, finish) in enumerate(jobs):
        y = y_next
        if j + 1 < len(jobs):
            y_next = project(jobs[j + 1][0])
        ref[0, :, col:col + MXU_COLS] = finish(y).astype(ref.dtype)


def _rope(y, cos, sin_signed, quarter):
    n = y.shape[-1]
    lane = lax.broadcasted_iota(jnp.int32, (1, n), 1)
    first = (lane & quarter) == 0
    partner = jnp.where(first, pltpu.roll(y, n - quarter, 1), pltpu.roll(y, quarter, 1))
    return y * cos + partner * sin_signed


def _adaln_kernel(cond_ref, w_ref, b_ref, o_ref):
    c = cond_ref[...]
    a = (c * jax.nn.sigmoid(c)).astype(BF16)
    o_ref[0] = _dot(a, w_ref[0].astype(BF16)) + b_ref[0]


def _adaln(cond, w_mod, b_mod):
    depth, d, n = w_mod.shape
    rows = cond.shape[0]
    tn = 1536
    return pl.pallas_call(
        _adaln_kernel,
        out_shape=jax.ShapeDtypeStruct((depth, rows, n), F32),
        grid=(depth, n // tn),
        in_specs=[
            pl.BlockSpec((rows, d), lambda l, j: (0, 0)),
            pl.BlockSpec((1, d, tn), lambda l, j: (l, 0, j)),
            pl.BlockSpec((1, 1, tn), lambda l, j: (l, 0, j)),
        ],
        out_specs=pl.BlockSpec((1, rows, tn), lambda l, j: (l, 0, j)),
        compiler_params=_params("parallel", "parallel"),
        name="adaln",
    )(cond, w_mod, b_mod.reshape(depth, 1, n))


def _mod_spec(mod):
    if mod.shape[0] == 1:
        return pl.BlockSpec((1,) + mod.shape[1:], lambda b, i: (0, 0, 0))
    return pl.BlockSpec((1,) + mod.shape[1:], lambda b, i: (b, 0, 0))


def _qkv_kernel(*refs, nq, nk, rope):
    x_ref, g_ref, mod_ref, w_ref, qg_ref, kg_ref, bd_ref = refs[:7]
    if rope:
        cos, sin = refs[7][...], refs[8][...]
        q_ref, k_ref, v_ref = refs[9:]
    else:
        cos = sin = None
        q_ref, k_ref, v_ref = refs[7:]
    h = _norm_mod(x_ref[0], g_ref[...], mod_ref[0, 0:1, :], mod_ref[0, 1:2, :]).astype(BF16)
    bd = bd_ref[...]

    def head(y, gain):
        y = _group_rms(y, gain, bd, HEAD_DIM)
        return _rope(y, cos, sin, HEAD_DIM // 4) if rope else y

    jobs = [(c, q_ref, c, lambda y: head(y, qg_ref[...])) for c in range(0, nq, MXU_COLS)]
    jobs += [(nq + c, k_ref, c, lambda y: head(y, kg_ref[...])) for c in range(0, nk, MXU_COLS)]
    jobs += [(nq + nk + c, v_ref, c, lambda y: y) for c in range(0, nk, MXU_COLS)]
    _pipelined_blocks(lambda col: _dot(h, w_ref[:, col:col + MXU_COLS]), jobs)


def _qkv_project(x, g, mod, w, q_gain, k_gain, nq, nk, kv_dtype, rope_tables=None):
    b, l, d = x.shape
    tl = min(TOKEN_TILE, l)
    rope = rope_tables is not None
    bd = _block_diag_ones(MXU_COLS, HEAD_DIM)
    args = [x, g.reshape(1, d), mod, w, _tile_gain(q_gain), _tile_gain(k_gain), bd]
    in_specs = [
        pl.BlockSpec((1, tl, d), lambda b_, i: (b_, i, 0)),
        _resident((1, d)),
        _mod_spec(mod),
        _resident(w.shape),
        _resident((1, MXU_COLS)),
        _resident((1, MXU_COLS)),
        _resident(bd.shape),
    ]
    if rope:
        args += list(rope_tables)
        in_specs += [pl.BlockSpec((tl, MXU_COLS), lambda b_, i: (i, 0))] * 2
    out = lambda n, dt: (jax.ShapeDtypeStruct((b, l, n), dt),
                         pl.BlockSpec((1, tl, n), lambda b_, i: (b_, i, 0)))
    shapes, specs = zip(out(nq, BF16), out(nk, kv_dtype), out(nk, kv_dtype))
    return pl.pallas_call(
        functools.partial(_qkv_kernel, nq=nq, nk=nk, rope=rope),
        out_shape=shapes,
        grid=(b, l // tl),
        in_specs=in_specs,
        out_specs=specs,
        compiler_params=_params("parallel", "parallel"),
        name="qkv_project",
    )(*args)


def _mla_project_kernel(*refs, rope):
    (x_ref, g_ref, mod_ref, wdq_ref, qlg_ref, wuq_ref, qng_ref, qpg_ref, bdn_ref, bdp_ref,
     wdkv_ref, kvg_ref, kpg_ref, wukv_ref, kng_ref, tile_ref) = refs[:16]
    if rope:
        cos, sin = refs[16][...], refs[17][...]
        qn_ref, qp_ref, ckv_ref, kpe_ref, kn_ref, v_ref, kpt_ref = refs[18:]
    else:
        cos = sin = None
        qn_ref, qp_ref, ckv_ref, kpe_ref, kn_ref, v_ref, kpt_ref = refs[16:]
    h = _norm_mod(x_ref[0], g_ref[...], mod_ref[0, 0:1, :], mod_ref[0, 1:2, :]).astype(BF16)
    q_lora = wdq_ref.shape[1]
    ql = _row_rms(_dot(h, wdq_ref[...]), qlg_ref[...], q_lora).astype(BF16)
    n_nope = qn_ref.shape[2]
    n_pe = qp_ref.shape[2]
    def pe_head(y):
        y = _group_rms(y, qpg_ref[...], bdp_ref[...], MLA_ROPE)
        return _rope(y, cos, sin, MLA_ROPE // 4) if rope else y

    jobs = [(c, qn_ref, c, lambda y: _group_rms(y, qng_ref[...], bdn_ref[...], MLA_NOPE))
            for c in range(0, n_nope, MXU_COLS)]
    jobs += [(n_nope + c, qp_ref, c, pe_head) for c in range(0, n_pe, MXU_COLS)]
    _pipelined_blocks(lambda col: _dot(ql, wuq_ref[:, col:col + MXU_COLS]), jobs)
    ckv_kpe = _dot(h, wdkv_ref[...])
    ckv = _row_rms(ckv_kpe[:, :MLA_KV_LORA], kvg_ref[...], MLA_KV_LORA)
    ckv_ref[0] = ckv
    kpe = _row_rms(ckv_kpe[:, MLA_KV_LORA:], kpg_ref[...], MLA_ROPE)
    if rope:
        kpe = _rope(kpe, cos[:, :LANES], sin[:, :LANES], MLA_ROPE // 4)
    kpe_ref[0] = kpe
    _mla_expand_blocks(ckv.astype(BF16), kpe.astype(BF16), wukv_ref, kng_ref, bdn_ref, tile_ref,
                       kn_ref, v_ref, kpt_ref)


def _mla_expand_blocks(c, kpe, w_ref, kng_ref, bd_ref, tile_ref, kn_ref, v_ref, kpt_ref):
    n = kn_ref.shape[2]
    jobs = [(col, kn_ref, col, lambda y: _group_rms(y, kng_ref[...], bd_ref[...], MLA_NOPE))
            for col in range(0, n, MXU_COLS)]
    jobs += [(n + col, v_ref, col, lambda y: y) for col in range(0, n, MXU_COLS)]
    _pipelined_blocks(lambda col: _dot(c, w_ref[:, col:col + MXU_COLS]), jobs)
    kpt_ref[0] = _dot(kpe, tile_ref[...]).astype(kpt_ref.dtype)


def _rope_key_tiler(width):
    tile = np.zeros((width, LANES), np.float32)
    for r in range(LANES // MLA_ROPE):
        tile[np.arange(MLA_ROPE), r * MLA_ROPE + np.arange(MLA_ROPE)] = 1.0
    return jnp.asarray(tile, BF16)


def _mla_project(x, g, mod, w_dq, q_lora_g, w_uq, q_nope_g, q_pe_g, w_dkv, kv_lora_g, k_pe_g, w_ukv,
                 k_nope_g, rope_tables=None):
    b, l, d = x.shape
    tl = min(TOKEN_TILE, l)
    rope = rope_tables is not None
    n_nope, n_pe = N_HEADS * MLA_NOPE, N_HEADS * MLA_ROPE
    bdn = _block_diag_ones(MXU_COLS, MLA_NOPE)
    bdp = _block_diag_ones(MXU_COLS, MLA_ROPE)
    kpg = jnp.pad(k_pe_g, (0, LANES - MLA_ROPE)).reshape(1, LANES)
    args = [x, g.reshape(1, d), mod, w_dq, q_lora_g.reshape(1, -1), w_uq, _tile_gain(q_nope_g),
            _tile_gain(q_pe_g), bdn, bdp, w_dkv, kv_lora_g.reshape(1, -1), kpg, w_ukv,
            _tile_gain(k_nope_g), _rope_key_tiler(LANES)]
    in_specs = [pl.BlockSpec((1, tl, d), lambda b_, i: (b_, i, 0)), _resident((1, d)), _mod_spec(mod)]
    in_specs += [_resident(a.shape) for a in args[3:]]
    if rope:
        args += list(rope_tables)
        in_specs += [pl.BlockSpec((tl, MXU_COLS), lambda b_, i: (i, 0))] * 2
    out = lambda n, dt: (jax.ShapeDtypeStruct((b, l, n), dt),
                         pl.BlockSpec((1, tl, n), lambda b_, i: (b_, i, 0)))
    shapes, specs = zip(out(n_nope, BF16), out(n_pe, BF16), out(MLA_KV_LORA, F32), out(LANES, F32),
                        out(N_HEADS * MLA_NOPE, BF16), out(N_HEADS * MLA_V, BF16), out(LANES, BF16))
    return pl.pallas_call(
        functools.partial(_mla_project_kernel, rope=rope),
        out_shape=shapes,
        grid=(b, l // tl),
        in_specs=in_specs,
        out_specs=specs,
        compiler_params=_params("parallel", "parallel"),
        name="mla_project",
    )(*args)


def _mla_expand_kernel(ckv_ref, kpe_ref, w_ref, kng_ref, bd_ref, tile_ref, kn_ref, v_ref, kpt_ref):
    _mla_expand_blocks(ckv_ref[0].astype(BF16), kpe_ref[0].astype(BF16), w_ref, kng_ref, bd_ref, tile_ref,
                       kn_ref, v_ref, kpt_ref)


def _mla_expand(ckv, kpe, w_ukv, k_nope_g):
    b, l, _ = ckv.shape
    tl = min(TOKEN_TILE, l)
    n = N_HEADS * MLA_NOPE
    pw = kpe.shape[2]
    bd = _block_diag_ones(MXU_COLS, MLA_NOPE)
    tile = _rope_key_tiler(pw)
    out = lambda m: (jax.ShapeDtypeStruct((b, l, m), BF16), pl.BlockSpec((1, tl, m), lambda b_, i: (b_, i, 0)))
    shapes, specs = zip(out(n), out(n), out(LANES))
    return pl.pallas_call(
        _mla_expand_kernel,
        out_shape=shapes,
        grid=(b, l // tl),
        in_specs=[
            pl.BlockSpec((1, tl, MLA_KV_LORA), lambda b_, i: (b_, i, 0)),
            pl.BlockSpec((1, tl, pw), lambda b_, i: (b_, i, 0)),
            _resident(w_ukv.shape),
            _resident((1, MXU_COLS)),
            _resident(bd.shape),
            _resident(tile.shape),
        ],
        out_specs=specs,
        compiler_params=_params("parallel", "parallel"),
        name="mla_expand",
    )(ckv, kpe, w_ukv, _tile_gain(k_nope_g), bd, tile)


def _key_norm_kernel(k_ref, o_ref, *, group):
    x = k_ref[0].astype(F32)
    sq = x * x
    width = min(LANES, sq.shape[1])
    lane = lax.broadcasted_iota(jnp.int32, (1, width), 1)
    best = None
    for c in range(0, sq.shape[1], width):
        tile = sq[:, c:c + width]
        for g0 in range(0, width, group):
            part = tile if group == width else jnp.where((lane >= g0) & (lane < g0 + group), tile, 0.0)
            top = jnp.max(jnp.sum(part, axis=-1, keepdims=True), axis=0, keepdims=True)
            best = top if best is None else jnp.maximum(best, top)
    o_ref[0] = jnp.broadcast_to(best, o_ref.shape[1:])


def _key_norm_max(k, group):
    b, l, n = k.shape
    out = pl.pallas_call(
        functools.partial(_key_norm_kernel, group=group),
        out_shape=jax.ShapeDtypeStruct((b, 8, LANES), F32),
        grid=(b,),
        in_specs=[pl.BlockSpec((1, l, n), lambda b_: (b_, 0, 0))],
        out_specs=pl.BlockSpec((1, 8, LANES), lambda b_: (b_, 0, 0)),
        compiler_params=_params("parallel"),
        name="key_norm_max",
    )(k)
    return jnp.sqrt(jnp.max(out))


def _attn_kernel(*refs, mode, ctx, mla, sink, tq, rows, pp, group, bounded, ctx_bounded):
    it = iter(refs)
    q_ref, k_ref, v_ref = next(it), next(it), next(it)
    kc_ref, vc_ref = (next(it), next(it)) if ctx else (None, None)
    q2_ref, k2_ref = (next(it), next(it)) if mla else (None, None)
    kc2_ref = next(it) if (mla and ctx) else None
    sink_ref = next(it) if sink else None
    tbl_ref = next(it) if mode in ("na", "swa") else None
    bound_ref = next(it) if bounded else None
    o_ref = next(it)

    g = pl.program_id(1)
    t = pl.program_id(2)
    lane = lax.broadcasted_iota(jnp.int32, (1, LANES), 1)
    plans = [_attn_pair_plan(i, g, t, lane, q_ref, k_ref, v_ref, kc_ref, vc_ref, q2_ref, k2_ref, kc2_ref,
                             sink_ref, tbl_ref, mode=mode, ctx=ctx, mla=mla, sink=sink, tq=tq, rows=rows,
                             pp=pp, group=group, ctx_bounded=ctx_bounded)
             for i in range(pp)]
    floor = bound_ref[0] if bounded else None

    def scores(i, j):
        kk, _, fix, _ = plans[i][1][j]
        s = _dot(kk, plans[i][0])
        return s if fix is None else fix(s)

    n_chunks = len(plans[0][1])
    state = [None] * pp
    s_next = [scores(i, 0) for i in range(pp)]
    for j in range(n_chunks):
        for i in range(pp):
            s = s_next[i]
            if j + 1 < n_chunks:
                s_next[i] = scores(i, j + 1)
            _, vv, _, exact_max = plans[i][1][j]
            state[i] = _softmax_step(state[i], s, vv, plans[i][2], floor, exact_max or not bounded)
    for i in range(pp):
        m, den, acc = state[i]
        if sink:
            den = den + jnp.exp2(plans[i][2] - m)
        o = acc / den
        o_pair = jnp.concatenate([o[:HEAD_DIM, :tq], o[HEAD_DIM:, tq:]], axis=0)
        o_ref[0, :, i * LANES:(i + 1) * LANES] = o_pair.T.astype(o_ref.dtype)


def _softmax_step(state, s, vv, sk, floor, exact_max):
    def probs(m):
        pr = jnp.exp2(s - m)
        return pr.astype(BF16), jnp.sum(pr, axis=0, keepdims=True)

    if state is None:
        m = None
        if floor is not None:
            m = jnp.full((1, s.shape[1]), floor, F32)
        if sk is not None:
            m = sk if m is None else jnp.maximum(m, sk)
        if exact_max:
            m_c = jnp.max(s, axis=0, keepdims=True)
            m = m_c if m is None else jnp.maximum(m, m_c)
        pr, total = probs(m)
        return m, total, _dot_tn(vv, pr)
    m, den, acc = state
    if exact_max:
        m_new = jnp.maximum(m, jnp.max(s, axis=0, keepdims=True))
        alpha = jnp.exp2(m - m_new)
        pr, total = probs(m_new)
        return m_new, alpha * den + total, alpha * acc + _dot_tn(vv, pr)
    pr, total = probs(m)
    return m, den + total, acc + _dot_tn(vv, pr)


def _attn_pair_plan(i, g, t, lane, q_ref, k_ref, v_ref, kc_ref, vc_ref, q2_ref, k2_ref, kc2_ref, sink_ref,
                    tbl_ref, *, mode, ctx, mla, sink, tq, rows, pp, group, ctx_bounded):
    ql = slice(i * LANES, (i + 1) * LANES)
    kvi = i // group if pp >= group else 0
    kl = slice(kvi * LANES, (kvi + 1) * LANES)

    row = lax.broadcasted_iota(jnp.int32, (LANES, 1), 0)

    def stack(x, m0, m1):
        xt = x.astype(F32).T
        return jnp.concatenate([jnp.where(m0, xt, 0.0), jnp.where(m1, xt, 0.0)], axis=1).astype(BF16)

    qs = stack(q_ref[0, :, ql], row < HEAD_DIM, row >= HEAD_DIM)
    if mla:
        if pp == 1:
            q2, base = q2_ref[0], (g % 2) * (2 * MLA_ROPE)
        else:
            q2, base = q2_ref[0, :, (i // 2) * LANES:(i // 2 + 1) * LANES], (i % 2) * (2 * MLA_ROPE)
        m0 = (row >= base) & (row < base + MLA_ROPE)
        m1 = (row >= base + MLA_ROPE) & (row < base + 2 * MLA_ROPE)
        qs = jnp.concatenate([qs, stack(q2, m0, m1)], axis=0)

    def keys(ref, ref2, rows_):
        kk = ref[0, rows_, kl].astype(BF16)
        if mla:
            kk = jnp.concatenate([kk, ref2[0, rows_, :].astype(BF16)], axis=1)
        return kk

    chunks = []
    if ctx:
        lc = kc_ref.shape[1]
        kc_len = min(KEY_CHUNK, lc)
        for c in range(0, lc, kc_len):
            rows_ = slice(c, c + kc_len)
            chunks.append((keys(kc_ref, kc2_ref, rows_), vc_ref[0, rows_, kl].astype(BF16), None,
                           not ctx_bounded))
    if mode == "full":
        lk = k_ref.shape[1]
        kc_len = min(KEY_CHUNK, lk)
        for c in range(0, lk, kc_len):
            rows_ = slice(c, c + kc_len)
            chunks.append((keys(k_ref, k2_ref, rows_), v_ref[0, rows_, kl].astype(BF16), None, False))
    elif mode == "swa":
        win = tq + 2 * SWA_WINDOW
        ws = pl.multiple_of(jnp.clip(t * tq - SWA_WINDOW, 0, k_ref.shape[1] - win), SWA_WINDOW)

        def band(s):
            return s + tbl_ref[(t * tq - ws) // SWA_WINDOW]

        chunks.append((k_ref[0, pl.ds(ws, win), kl], v_ref[0, pl.ds(ws, win), kl], band, False))
    else:
        qrows = tq // GRID_W
        wrows = qrows + NA_WIN_ROWS - 1
        r0 = t * qrows
        ws_row = jnp.clip(r0 - NA_WIN_ROWS // 2, 0, rows - wrows)
        ws = pl.multiple_of(ws_row * GRID_W, GRID_W)

        def window_bias(s):
            e0 = ws_row - r0 + (NA_WIN_ROWS - 1) + qrows - 1
            out_rows = []
            for j in range(wrows):
                kr = ws_row + j
                blocks = []
                for hh in range(2):
                    for ii in range(qrows // 2):
                        ok = []
                        for qr in (r0 + 2 * ii, r0 + 2 * ii + 1):
                            rs = jnp.clip(qr - NA_WIN_ROWS // 2, 0, rows - NA_WIN_ROWS)
                            ok.append(((kr >= rs) & (kr < rs + NA_WIN_ROWS)).astype(jnp.int32))
                        valid = jnp.where(lane < GRID_W, ok[0], ok[1]) > 0
                        lo = hh * tq + ii * LANES
                        blk = (s[j * GRID_W:(j + 1) * GRID_W, lo:lo + LANES]
                               + tbl_ref[2 * i + hh, e0 + j - 2 * ii])
                        blocks.append(jnp.where(valid, blk, NEG_INF))
                out_rows.append(jnp.concatenate(blocks, axis=1))
            return jnp.concatenate(out_rows, axis=0)

        chunks.append((k_ref[0, pl.ds(ws, wrows * GRID_W), kl], v_ref[0, pl.ds(ws, wrows * GRID_W), kl],
                       window_bias, False))

    sk = None
    if sink:
        first = lax.broadcasted_iota(jnp.int32, (1, 2 * tq), 1) < tq
        head = 2 * (g * pp + i)
        sk = jnp.where(first, sink_ref[head], sink_ref[head + 1])
    return qs, chunks, sk


def _attention(q, k, v, *, mode="full", kc=None, vc=None, q2=None, k2=None, kc2=None, sink=None,
               na_table=None, bound=None, ctx_bounded=False, pp=1, q_tile=Q_TILE):
    b, lq, nq = q.shape
    lk = k.shape[1]
    pairs = nq // LANES
    group = pairs // (k.shape[2] // LANES)
    tq = min(q_tile, lq)
    ctx, mla = kc is not None, q2 is not None
    assert pairs % pp == 0 and (pp % group == 0 or group % pp == 0)
    assert not mla or pp == 1 or pp % 2 == 0

    kv_lanes = LANES * max(1, pp // group)
    qmap = lambda b_, g, t: (b_, t, g)
    if pp >= group:
        kvmap = lambda b_, g, t: (b_, 0, g)
    else:
        kvmap = lambda b_, g, t: (b_, 0, (g * pp) // group)
    shared = lambda b_, g, t: (b_, 0, 0)
    args = [q, k, v]
    in_specs = [pl.BlockSpec((1, tq, pp * LANES), qmap),
                pl.BlockSpec((1, lk, kv_lanes), kvmap),
                pl.BlockSpec((1, lk, kv_lanes), kvmap)]
    if ctx:
        lc = kc.shape[1]
        args += [kc, vc]
        in_specs += [pl.BlockSpec((1, lc, kv_lanes), kvmap)] * 2
    if mla:
        args += [q2, k2]
        if pp == 1:
            in_specs += [pl.BlockSpec((1, tq, LANES), lambda b_, g, t: (b_, t, g // 2))]
        else:
            in_specs += [pl.BlockSpec((1, tq, pp * LANES // 2), qmap)]
        in_specs += [pl.BlockSpec((1, lk, LANES), shared)]
        if ctx:
            args += [kc2]
            in_specs += [pl.BlockSpec((1, kc2.shape[1], LANES), shared)]
    if sink is not None:
        args += [sink]
        in_specs += [pl.BlockSpec(memory_space=pltpu.SMEM)]
    if mode == "na":
        args += [na_table]
        in_specs += [pl.BlockSpec((2 * pp,) + na_table.shape[1:], lambda b_, g, t: (g, 0, 0, 0))]
    if mode == "swa":
        assert tq % SWA_WINDOW == 0
        mask = _swa_band_masks(tq)
        args += [mask]
        in_specs += [_resident(mask.shape)]
    if bound is not None:
        args += [bound]
        in_specs += [pl.BlockSpec(memory_space=pltpu.SMEM)]
    return pl.pallas_call(
        functools.partial(_attn_kernel, mode=mode, ctx=ctx, mla=mla, sink=sink is not None, tq=tq,
                          rows=lq // GRID_W, pp=pp, group=group, bounded=bound is not None,
                          ctx_bounded=ctx_bounded and bound is not None),
        out_shape=jax.ShapeDtypeStruct((b, lq, nq), BF16),
        grid=(b, pairs // pp, lq // tq),
        in_specs=in_specs,
        out_specs=pl.BlockSpec((1, tq, pp * LANES), qmap),
        compiler_params=_params("parallel", "parallel", "parallel"),
        name="attention_" + mode,
    )(*args)


def _attention_guarded(score_bound, *args, **kwargs):
    bound = jnp.reshape(score_bound * BOUND_SLACK, (1,)).astype(F32)
    return lax.cond(bound[0] <= MAX_SCORE_BOUND,
                    lambda: _attention(*args, bound=bound, **kwargs),
                    lambda: _attention(*args, **kwargs))


def _out_ffn_kernel(x_ref, a_ref, mod_ref, g_ref, wo_ref, wg_ref, wu_ref, wd_ref, o_ref, *, chunks):
    mod = lambda r: mod_ref[0, r:r + 1, :]
    x1 = x_ref[0] + mod(2) * _dot(a_ref[0], wo_ref[...])
    h = _norm_mod(x1, g_ref[...], mod(3), mod(4)).astype(BF16)
    hidden = wg_ref.shape[1]
    step = hidden // chunks
    gate_up = lambda c: (_dot(h, wg_ref[:, c:c + step]), _dot(h, wu_ref[:, c:c + step]))
    y = None
    nxt = gate_up(0)
    for c in range(0, hidden, step):
        gate, up = nxt
        if c + step < hidden:
            nxt = gate_up(c + step)
        act = (gate * jax.nn.sigmoid(gate) * up).astype(BF16)
        part = _dot(act, wd_ref[c:c + step, :])
        y = part if y is None else y + part
    o_ref[0] = x1 + mod(5) * y


def _out_ffn(x, attn, mod, g2, w_o, w_gate, w_up, w_down, layer):
    b, l, d = x.shape
    tl = min(TOKEN_TILE, l)
    tok = lambda n: pl.BlockSpec((1, tl, n), lambda b_, i: (b_, i, 0))
    of_layer = lambda w: pl.BlockSpec((None,) + w.shape[1:], lambda b_, i: (layer, 0, 0),
                                      pipeline_mode=pl.Buffered(1))
    return pl.pallas_call(
        functools.partial(_out_ffn_kernel, chunks=11),
        out_shape=jax.ShapeDtypeStruct((b, l, d), F32),
        grid=(b, l // tl),
        in_specs=[tok(d), tok(attn.shape[2]), _mod_spec(mod), _resident((1, d)),
                  _resident(w_o.shape), of_layer(w_gate), of_layer(w_up), of_layer(w_down)],
        out_specs=tok(d),
        compiler_params=_params("parallel", "parallel"),
        name="out_ffn",
    )(x, attn, mod, g2.reshape(1, d), w_o, w_gate, w_up, w_down)


def _swa_band_masks(tq):
    i = np.arange(tq + 2 * SWA_WINDOW)[None, :, None]
    j = (np.arange(2 * tq) % tq)[None, None, :]
    delta = (np.arange(3) * SWA_WINDOW)[:, None, None]
    return jnp.asarray(np.where(np.abs(i - delta - j) <= SWA_WINDOW, 0.0, NEG_INF), F32)


def _max_abs(a):
    return jnp.max(jnp.abs(a))


def _block_diag_ones(n, group):
    idx = np.arange(n) // group
    return jnp.asarray(idx[:, None] == idx[None, :], BF16)


def _tile_gain(g, scale=1.0):
    return jnp.tile(g * scale, MXU_COLS // g.shape[0]).reshape(1, MXU_COLS)


def _rope_tables(n_tokens, r):
    quarter = r // 4
    n_rows = n_tokens // GRID_W
    inv_freq = ROPE_THETA ** (-jnp.arange(quarter, dtype=F32) / quarter)
    ang = jnp.arange(max(n_rows, GRID_W), dtype=F32)[:, None] * inv_freq[None, :]
    by_row = lambda a: jnp.broadcast_to(a[:n_rows, None, :], (n_rows, GRID_W, quarter)).reshape(n_tokens, quarter)
    by_col = lambda a: jnp.broadcast_to(a[None, :GRID_W, :], (n_rows, GRID_W, quarter)).reshape(n_tokens, quarter)
    cos, sin = jnp.cos(ang), jnp.sin(ang)
    cos_head = jnp.concatenate([by_row(cos), by_row(cos), by_col(cos), by_col(cos)], axis=-1)
    sin_head = jnp.concatenate([-by_row(sin), by_row(sin), -by_col(sin), by_col(sin)], axis=-1)
    return jnp.tile(cos_head, (1, MXU_COLS // r)), jnp.tile(sin_head, (1, MXU_COLS // r))


_GQA_HEADS = np.array([8 * (p // 4) + 4 * e + (p % 4) for p in range(N_HEADS // 2) for e in range(2)])
_GQA_COLS = (_GQA_HEADS[:, None] * HEAD_DIM + np.arange(HEAD_DIM)[None, :]).reshape(-1)


def _na_bias_table(rpb, q_rows):
    n_heads, _, n_dc = rpb.shape
    qc = np.arange(LANES)[None, :] % GRID_W
    kc = np.arange(GRID_W)[:, None]
    start = np.clip(qc - NA_WIN_COLS // 2, 0, GRID_W - NA_WIN_COLS)
    valid = (kc >= start) & (kc < start + NA_WIN_COLS)
    dcol = np.clip(kc - qc + NA_WIN_COLS - 1, 0, n_dc - 1) + n_dc * (np.arange(LANES)[None, :] // GRID_W)
    onehot = (np.arange(2 * n_dc)[:, None] == dcol.reshape(1, -1)).astype(np.float32)
    n_e = 2 * q_rows + 2 * NA_WIN_ROWS - 2
    rp = jnp.pad(rpb, ((0, 0), (q_rows, q_rows), (0, 0)))
    pairs = jnp.concatenate([rp[:, 1:1 + n_e], rp[:, 0:n_e]], axis=-1)
    t = jnp.dot(pairs.reshape(n_heads * n_e, 2 * n_dc), onehot, precision=lax.Precision.HIGHEST)
    return jnp.where(valid[None, None], t.reshape(n_heads, n_e, GRID_W, LANES), NEG_INF)


def kernel(x_prompt, x_sample, cache_na_k, cache_na_v, cache_swa_k, cache_swa_v, cache_mla_ckv, cache_mla_kpe, cache_gqa_k, cache_gqa_v, c, c_ctx, norm1_g, norm2_g, w_mod, b_mod, na_w_qkv, na_q_g, na_k_g, na_rpb, na_w_o, swa_w_qkv, swa_q_g, swa_k_g, swa_sink, swa_w_o, mla_w_dq, mla_q_lora_g, mla_w_uq, mla_q_nope_g, mla_q_pe_g, mla_w_dkv, mla_kv_lora_g, mla_k_pe_g, mla_w_ukv, mla_k_nope_g, mla_w_o, gqa_w_qkv, gqa_q_g, gqa_k_g, gqa_w_o, ffn_w_gate, ffn_w_up, ffn_w_down):
    depth, d = norm1_g.shape
    bp, lp, _ = x_prompt.shape
    bs, ls, _ = x_sample.shape
    hq, hkv = N_HEADS * HEAD_DIM, N_KV_HEADS * HEAD_DIM
    scale = HEAD_DIM ** -0.5 * LOG2E

    cond = jnp.concatenate([c_ctx[None, :], c, jnp.zeros((16 - 1 - bs, d), F32)], axis=0)
    mods = _adaln(cond, w_mod, b_mod).reshape(depth, 16, 6, d)
    rope64 = _rope_tables(ls, HEAD_DIM)
    rope32 = _rope_tables(ls, MLA_ROPE)
    flat = lambda a, r: a[:, r].reshape(a.shape[0], a.shape[2], -1)

    wg, wu, wd = ffn_w_gate.astype(BF16), ffn_w_up.astype(BF16), ffn_w_down.astype(BF16)
    fold_n = max(1, TOKEN_TILE // lp)
    fold_n = fold_n if bp % fold_n == 0 else 1
    fold = lambda a: a.reshape(a.shape[0] // fold_n, fold_n * a.shape[1], a.shape[2])
    unfold = lambda a: lax.optimization_barrier(a.reshape(bp, lp, a.shape[2]))
    xp, xs = fold(x_prompt), x_sample
    outs = {}
    for li in range(depth):
        kind, r = li % 4, li // 4
        mp, ms = mods[li, 0:1], mods[li, 1:1 + bs]
        g1 = norm1_g[li]
        if kind in (0, 1, 3):
            w_qkv, q_g, k_g, w_o = ((na_w_qkv, na_q_g, na_k_g, na_w_o), (swa_w_qkv, swa_q_g, swa_k_g, swa_w_o),
                                    None, (gqa_w_qkv, gqa_q_g, gqa_k_g, gqa_w_o))[kind]
            w_qkv, w_o = w_qkv[r], w_o[r]
            nk = hq if kind == 0 else hkv
            if kind != 0:
                w_qkv = jnp.concatenate([w_qkv[:, :hq][:, _GQA_COLS], w_qkv[:, hq:]], axis=1)
                w_o = w_o[_GQA_COLS, :]
            w_qkv, w_o = w_qkv.astype(BF16), w_o.astype(BF16)
            qp, kp, vp = map(unfold, _qkv_project(xp, g1, mp, w_qkv, q_g[r] * scale, k_g[r], hq, nk, F32))
            qs, ks, vs = _qkv_project(xs, g1, ms, w_qkv, q_g[r] * scale, k_g[r], hq, nk, BF16,
                                      rope_tables=None if kind == 0 else rope64)
            kc, vc = [flat(a, r) for a in ((cache_na_k, cache_na_v), (cache_swa_k, cache_swa_v), None,
                                           (cache_gqa_k, cache_gqa_v))[kind]]
            q_norm = HEAD_DIM ** 0.5 * _max_abs(q_g[r] * scale)
            qk_bound = q_norm * HEAD_DIM ** 0.5 * _max_abs(k_g[r])
            ctx_bound = q_norm * _key_norm_max(kc, HEAD_DIM)
            if kind == 0:
                rpb = na_rpb[r] * LOG2E
                ap = _attention_guarded(qk_bound, qp, kp, vp, pp=PROMPT_PAIRS)
                as_ = _attention_guarded(jnp.maximum(qk_bound + _max_abs(rpb), ctx_bound), qs, ks, vs, mode="na",
                                         kc=kc, vc=vc, na_table=_na_bias_table(rpb, WINDOW_Q_TILE // GRID_W),
                                         ctx_bounded=True, pp=WINDOW_PAIRS, q_tile=WINDOW_Q_TILE)
                outs["na_k"], outs["na_v"] = kp, vp
            elif kind == 1:
                sink = swa_sink[r][_GQA_HEADS] * LOG2E
                ap = _attention_guarded(qk_bound, qp, kp, vp, sink=sink, pp=PROMPT_PAIRS)
                as_ = _attention_guarded(jnp.maximum(qk_bound, ctx_bound), qs, ks, vs, mode="swa", kc=kc, vc=vc,
                                         sink=sink, ctx_bounded=True, pp=WINDOW_PAIRS, q_tile=WINDOW_Q_TILE)
                outs["swa_k"], outs["swa_v"] = kp, vp
            else:
                ap = _attention_guarded(qk_bound, qp, kp, vp, pp=PROMPT_PAIRS)
                as_ = _attention_guarded(jnp.maximum(qk_bound, ctx_bound), qs, ks, vs, kc=kc, vc=vc,
                                         ctx_bounded=True, pp=LATENT_PAIRS)
                outs["gqa_k"], outs["gqa_v"] = kp, vp
        else:
            dk = MLA_NOPE + MLA_ROPE
            w_uq = mla_w_uq[r].reshape(-1, N_HEADS, dk)
            w_uq = jnp.concatenate([w_uq[:, :, :MLA_NOPE].reshape(-1, N_HEADS * MLA_NOPE),
                                    w_uq[:, :, MLA_NOPE:].reshape(-1, N_HEADS * MLA_ROPE)], axis=1).astype(BF16)
            w_ukv = mla_w_ukv[r].reshape(-1, N_HEADS, MLA_NOPE + MLA_V)
            w_ukv = jnp.concatenate([w_ukv[:, :, :MLA_NOPE].reshape(-1, N_HEADS * MLA_NOPE),
                                     w_ukv[:, :, MLA_NOPE:].reshape(-1, N_HEADS * MLA_V)], axis=1).astype(BF16)
            w_dkv = jnp.pad(mla_w_dkv[r], ((0, 0), (0, 2 * LANES - MLA_KV_LORA - MLA_ROPE))).astype(BF16)
            w_dq, w_o = mla_w_dq[r].astype(BF16), mla_w_o[r].astype(BF16)
            qscale = dk ** -0.5 * LOG2E
            proj = functools.partial(
                _mla_project, w_dq=w_dq, q_lora_g=mla_q_lora_g[r], w_uq=w_uq, q_nope_g=mla_q_nope_g[r] * qscale,
                q_pe_g=mla_q_pe_g[r] * qscale, w_dkv=w_dkv, kv_lora_g=mla_kv_lora_g[r], k_pe_g=mla_k_pe_g[r],
                w_ukv=w_ukv, k_nope_g=mla_k_nope_g[r])
            qn_p, qpe_p, ckv_p, kpe_p, kn_p, v_p, kpt_p = map(unfold, proj(xp, g1, mp))
            qn_s, qpe_s, _, _, kn_s, v_s, kpt_s = proj(xs, g1, ms, rope_tables=rope32)
            kn_c, v_c, kpt_c = _mla_expand(cache_mla_ckv[:, r], cache_mla_kpe[:, r], w_ukv, mla_k_nope_g[r])
            q_norm = jnp.sqrt(MLA_NOPE * _max_abs(mla_q_nope_g[r] * qscale) ** 2
                              + MLA_ROPE * _max_abs(mla_q_pe_g[r] * qscale) ** 2)
            k_norm = jnp.sqrt(MLA_NOPE * _max_abs(mla_k_nope_g[r]) ** 2 + MLA_ROPE * _max_abs(mla_k_pe_g[r]) ** 2)
            ctx_k_norm = jnp.sqrt(MLA_NOPE * _max_abs(mla_k_nope_g[r]) ** 2
                                  + _key_norm_max(cache_mla_kpe[:, r], MLA_ROPE) ** 2)
            ap = _attention_guarded(q_norm * k_norm, qn_p, kn_p, v_p, q2=qpe_p, k2=kpt_p, pp=PROMPT_PAIRS)
            as_ = _attention_guarded(q_norm * jnp.maximum(k_norm, ctx_k_norm), qn_s, kn_s, v_s, kc=kn_c, vc=v_c,
                                     q2=qpe_s, k2=kpt_s, kc2=kpt_c, ctx_bounded=True, pp=LATENT_PAIRS)
            outs["mla_ckv"], outs["mla_kpe"] = ckv_p, kpe_p[:, :, :MLA_ROPE]
        xp = _out_ffn(xp, fold(ap), mp, norm2_g[li], w_o, wg, wu, wd, li)
        xs = _out_ffn(xs, as_, ms, norm2_g[li], w_o, wg, wu, wd, li)

    heads = lambda a, h: a.reshape(bp, 1, lp, h, HEAD_DIM)
    return (unfold(xp), xs, heads(outs["na_k"], N_HEADS), heads(outs["na_v"], N_HEADS),
            heads(outs["swa_k"], N_KV_HEADS), heads(outs["swa_v"], N_KV_HEADS),
            outs["mla_ckv"][:, None], outs["mla_kpe"][:, None],
            heads(outs["gqa_k"], N_KV_HEADS), heads(outs["gqa_v"], N_KV_HEADS))
```

```python
import functools

import numpy as np
import jax
import jax.numpy as jnp
from jax import lax
from jax.experimental import pallas as pl
from jax.experimental.pallas import tpu as pltpu

GRID_W = 64
HEAD_DIM = 64
N_HEADS = 16
N_KV_HEADS = 4
NA_WIN_ROWS = 8
NA_WIN_COLS = 16
SWA_WINDOW = 128
MLA_KV_LORA = 128
MLA_NOPE = 64
MLA_ROPE = 32
MLA_V = 64
ROPE_THETA = 10000.0
EPS = 1e-6
NEG_INF = -1e30

LANES = 128
MXU_COLS = 256
VMEM_LIMIT = 56 * 1024 * 1024
TOKEN_TILE = 512
Q_TILE = 256
KEY_CHUNK = 256
LOG2E = 1.4426950408889634
PROMPT_PAIRS = 8
LATENT_PAIRS = 4
WINDOW_Q_TILE = 128
WINDOW_PAIRS = 8
BOUND_SLACK = 1.01
MAX_SCORE_BOUND = 48.0

F32 = jnp.float32
BF16 = jnp.bfloat16


def _params(*sem):
    return pltpu.CompilerParams(dimension_semantics=sem, vmem_limit_bytes=VMEM_LIMIT)


def _resident(shape):
    nd = len(shape)
    return pl.BlockSpec(shape, lambda *_: (0,) * nd, pipeline_mode=pl.Buffered(1))


def _dot(a, b):
    return jnp.dot(a, b, preferred_element_type=F32)


def _dot_nt(a, b):
    return lax.dot_general(a, b, (((1,), (1,)), ((), ())), preferred_element_type=F32)


def _dot_tn(a, b):
    return lax.dot_general(a, b, (((0,), (0,)), ((), ())), preferred_element_type=F32)


def _norm_mod(x, g, shift, scale):
    ms = jnp.mean(x * x, axis=-1, keepdims=True)
    y = x * lax.rsqrt(ms + EPS) * g
    return y * (1.0 + scale) + shift


def _row_rms(x, g, n):
    ms = jnp.sum(x * x, axis=-1, keepdims=True) * (1.0 / n)
    return x * lax.rsqrt(ms + EPS) * g


def _group_rms(y, gain, bd, group):
    ss = _dot((y * y).astype(BF16), bd)
    return y * lax.rsqrt(ss * (1.0 / group) + EPS) * gain


def _pipelined_blocks(project, jobs):
    y_next = project(jobs[0][0])
    for j, (_, ref, col, finish) in enumerate(jobs):
        y = y_next
        if j + 1 < len(jobs):
            y_next = project(jobs[j + 1][0])
        ref[0, :, col:col + MXU_COLS] = finish(y).astype(ref.dtype)


def _rope(y, cos, sin_signed, quarter):
    n = y.shape[-1]
    lane = lax.broadcasted_iota(jnp.int32, (1, n), 1)
    first = (lane & quarter) == 0
    partner = jnp.where(first, pltpu.roll(y, n - quarter, 1), pltpu.roll(y, quarter, 1))
    return y * cos + partner * sin_signed


def _adaln_kernel(cond_ref, w_ref, b_ref, o_ref):
    c = cond_ref[...]
    a = (c * jax.nn.sigmoid(c)).astype(BF16)
    o_ref[0] = _dot(a, w_ref[0].astype(BF16)) + b_ref[0]


def _adaln(cond, w_mod, b_mod):
    depth, d, n = w_mod.shape
    rows = cond.shape[0]
    tn = 1536
    return pl.pallas_call(
        _adaln_kernel,
        out_shape=jax.ShapeDtypeStruct((depth, rows, n), F32),
        grid=(depth, n // tn),
        in_specs=[
            pl.BlockSpec((rows, d), lambda l, j: (0, 0)),
            pl.BlockSpec((1, d, tn), lambda l, j: (l, 0, j)),
            pl.BlockSpec((1, 1, tn), lambda l, j: (l, 0, j)),
        ],
        out_specs=pl.BlockSpec((1, rows, tn), lambda l, j: (l, 0, j)),
        compiler_params=_params("parallel", "parallel"),
        name="adaln",
    )(cond, w_mod, b_mod.reshape(depth, 1, n))


def _mod_spec(mod):
    if mod.shape[0] == 1:
        return pl.BlockSpec((1,) + mod.shape[1:], lambda b, i: (0, 0, 0))
    return pl.BlockSpec((1,) + mod.shape[1:], lambda b, i: (b, 0, 0))


def _qkv_kernel(*refs, nq, nk, rope):
    x_ref, g_ref, mod_ref, w_ref, qg_ref, kg_ref, bd_ref = refs[:7]
    if rope:
        cos, sin = refs[7][...], refs[8][...]
        q_ref, k_ref, v_ref = refs[9:]
    else:
        cos = sin = None
        q_ref, k_ref, v_ref = refs[7:]
    h = _norm_mod(x_ref[0], g_ref[...], mod_ref[0, 0:1, :], mod_ref[0, 1:2, :]).astype(BF16)
    bd = bd_ref[...]

    def head(y, gain):
        y = _group_rms(y, gain, bd, HEAD_DIM)
        return _rope(y, cos, sin, HEAD_DIM // 4) if rope else y

    jobs = [(c, q_ref, c, lambda y: head(y, qg_ref[...])) for c in range(0, nq, MXU_COLS)]
    jobs += [(nq + c, k_ref, c, lambda y: head(y, kg_ref[...])) for c in range(0, nk, MXU_COLS)]
    jobs += [(nq + nk + c, v_ref, c, lambda y: y) for c in range(0, nk, MXU_COLS)]
    _pipelined_blocks(lambda col: _dot(h, w_ref[:, col:col + MXU_COLS]), jobs)


def _qkv_project(x, g, mod, w, q_gain, k_gain, nq, nk, kv_dtype, rope_tables=None):
    b, l, d = x.shape
    tl = min(TOKEN_TILE, l)
    rope = rope_tables is not None
    bd = _block_diag_ones(MXU_COLS, HEAD_DIM)
    args = [x, g.reshape(1, d), mod, w, _tile_gain(q_gain), _tile_gain(k_gain), bd]
    in_specs = [
        pl.BlockSpec((1, tl, d), lambda b_, i: (b_, i, 0)),
        _resident((1, d)),
        _mod_spec(mod),
        _resident(w.shape),
        _resident((1, MXU_COLS)),
        _resident((1, MXU_COLS)),
        _resident(bd.shape),
    ]
    if rope:
        args += list(rope_tables)
        in_specs += [pl.BlockSpec((tl, MXU_COLS), lambda b_, i: (i, 0))] * 2
    out = lambda n, dt: (jax.ShapeDtypeStruct((b, l, n), dt),
                         pl.BlockSpec((1, tl, n), lambda b_, i: (b_, i, 0)))
    shapes, specs = zip(out(nq, BF16), out(nk, kv_dtype), out(nk, kv_dtype))
    return pl.pallas_call(
        functools.partial(_qkv_kernel, nq=nq, nk=nk, rope=rope),
        out_shape=shapes,
        grid=(b, l // tl),
        in_specs=in_specs,
        out_specs=specs,
        compiler_params=_params("parallel", "parallel"),
        name="qkv_project",
    )(*args)


def _mla_project_kernel(*refs, rope, emit_compressed):
    (x_ref, g_ref, mod_ref, wdq_ref, qlg_ref, wuq_ref, qng_ref, qpg_ref, bdn_ref, bdp_ref,
     wdkv_ref, kvg_ref, kpg_ref, wukv_ref, kng_ref, tile_ref) = refs[:16]
    cos, sin = (refs[16][...], refs[17][...]) if rope else (None, None)
    outs = refs[18:] if rope else refs[16:]
    if emit_compressed:
        qn_ref, qp_ref, ckv_ref, kpe_ref, kn_ref, v_ref, kpt_ref = outs
    else:
        qn_ref, qp_ref, kn_ref, v_ref, kpt_ref = outs
    h = _norm_mod(x_ref[0], g_ref[...], mod_ref[0, 0:1, :], mod_ref[0, 1:2, :]).astype(BF16)
    q_lora = wdq_ref.shape[1]
    ql = _row_rms(_dot(h, wdq_ref[...]), qlg_ref[...], q_lora).astype(BF16)
    n_nope = qn_ref.shape[2]
    n_pe = qp_ref.shape[2]
    def pe_head(y):
        y = _group_rms(y, qpg_ref[...], bdp_ref[...], MLA_ROPE)
        return _rope(y, cos, sin, MLA_ROPE // 4) if rope else y

    jobs = [(c, qn_ref, c, lambda y: _group_rms(y, qng_ref[...], bdn_ref[...], MLA_NOPE))
            for c in range(0, n_nope, MXU_COLS)]
    jobs += [(n_nope + c, qp_ref, c, pe_head) for c in range(0, n_pe, MXU_COLS)]
    _pipelined_blocks(lambda col: _dot(ql, wuq_ref[:, col:col + MXU_COLS]), jobs)
    ckv_kpe = _dot(h, wdkv_ref[...])
    ckv = _row_rms(ckv_kpe[:, :MLA_KV_LORA], kvg_ref[...], MLA_KV_LORA)
    kpe = _row_rms(ckv_kpe[:, MLA_KV_LORA:], kpg_ref[...], MLA_ROPE)
    if rope:
        kpe = _rope(kpe, cos[:, :LANES], sin[:, :LANES], MLA_ROPE // 4)
    if emit_compressed:
        ckv_ref[0] = ckv
        kpe_ref[0] = kpe
    _mla_expand_blocks(ckv.astype(BF16), kpe.astype(BF16), wukv_ref, kng_ref, bdn_ref, tile_ref,
                       kn_ref, v_ref, kpt_ref)


def _mla_expand_blocks(c, kpe, w_ref, kng_ref, bd_ref, tile_ref, kn_ref, v_ref, kpt_ref):
    n = kn_ref.shape[2]
    jobs = [(col, kn_ref, col, lambda y: _group_rms(y, kng_ref[...], bd_ref[...], MLA_NOPE))
            for col in range(0, n, MXU_COLS)]
    jobs += [(n + col, v_ref, col, lambda y: y) for col in range(0, n, MXU_COLS)]
    _pipelined_blocks(lambda col: _dot(c, w_ref[:, col:col + MXU_COLS]), jobs)
    kpt_ref[0] = _dot(kpe, tile_ref[...]).astype(kpt_ref.dtype)


def _rope_key_tiler(width):
    tile = np.zeros((width, LANES), np.float32)
    for r in range(LANES // MLA_ROPE):
        tile[np.arange(MLA_ROPE), r * MLA_ROPE + np.arange(MLA_ROPE)] = 1.0
    return jnp.asarray(tile, BF16)


def _mla_project(x, g, mod, w_dq, q_lora_g, w_uq, q_nope_g, q_pe_g, w_dkv, kv_lora_g, k_pe_g, w_ukv,
                 k_nope_g, rope_tables=None, emit_compressed=True):
    b, l, d = x.shape
    tl = min(TOKEN_TILE, l)
    rope = rope_tables is not None
    n_nope, n_pe = N_HEADS * MLA_NOPE, N_HEADS * MLA_ROPE
    bdn = _block_diag_ones(MXU_COLS, MLA_NOPE)
    bdp = _block_diag_ones(MXU_COLS, MLA_ROPE)
    kpg = jnp.pad(k_pe_g, (0, LANES - MLA_ROPE)).reshape(1, LANES)
    args = [x, g.reshape(1, d), mod, w_dq, q_lora_g.reshape(1, -1), w_uq, _tile_gain(q_nope_g),
            _tile_gain(q_pe_g), bdn, bdp, w_dkv, kv_lora_g.reshape(1, -1), kpg, w_ukv,
            _tile_gain(k_nope_g), _rope_key_tiler(LANES)]
    in_specs = [pl.BlockSpec((1, tl, d), lambda b_, i: (b_, i, 0)), _resident((1, d)), _mod_spec(mod)]
    in_specs += [_resident(a.shape) for a in args[3:]]
    if rope:
        args += list(rope_tables)
        in_specs += [pl.BlockSpec((tl, MXU_COLS), lambda b_, i: (i, 0))] * 2
    out = lambda n, dt: (jax.ShapeDtypeStruct((b, l, n), dt),
                         pl.BlockSpec((1, tl, n), lambda b_, i: (b_, i, 0)))
    compressed = [out(MLA_KV_LORA, F32), out(LANES, F32)] if emit_compressed else []
    shapes, specs = zip(out(n_nope, BF16), out(n_pe, BF16), *compressed,
                        out(N_HEADS * MLA_NOPE, BF16), out(N_HEADS * MLA_V, BF16), out(LANES, BF16))
    return pl.pallas_call(
        functools.partial(_mla_project_kernel, rope=rope, emit_compressed=emit_compressed),
        out_shape=shapes,
        grid=(b, l // tl),
        in_specs=in_specs,
        out_specs=specs,
        compiler_params=_params("parallel", "parallel"),
        name="mla_project",
    )(*args)


def _mla_expand_kernel(ckv_ref, kpe_ref, w_ref, kng_ref, bd_ref, tile_ref, kn_ref, v_ref, kpt_ref):
    _mla_expand_blocks(ckv_ref[0].astype(BF16), kpe_ref[0].astype(BF16), w_ref, kng_ref, bd_ref, tile_ref,
                       kn_ref, v_ref, kpt_ref)


def _mla_expand(ckv, kpe, w_ukv, k_nope_g):
    b, l, _ = ckv.shape
    tl = min(TOKEN_TILE, l)
    n = N_HEADS * MLA_NOPE
    pw = kpe.shape[2]
    bd = _block_diag_ones(MXU_COLS, MLA_NOPE)
    tile = _rope_key_tiler(pw)
    out = lambda m: (jax.ShapeDtypeStruct((b, l, m), BF16), pl.BlockSpec((1, tl, m), lambda b_, i: (b_, i, 0)))
    shapes, specs = zip(out(n), out(n), out(LANES))
    return pl.pallas_call(
        _mla_expand_kernel,
        out_shape=shapes,
        grid=(b, l // tl),
        in_specs=[
            pl.BlockSpec((1, tl, MLA_KV_LORA), lambda b_, i: (b_, i, 0)),
            pl.BlockSpec((1, tl, pw), lambda b_, i: (b_, i, 0)),
            _resident(w_ukv.shape),
            _resident((1, MXU_COLS)),
            _resident(bd.shape),
            _resident(tile.shape),
        ],
        out_specs=specs,
        compiler_params=_params("parallel", "parallel"),
        name="mla_expand",
    )(ckv, kpe, w_ukv, _tile_gain(k_nope_g), bd, tile)


def _key_norm_kernel(k_ref, o_ref, *, group):
    x = k_ref[0].astype(F32)
    sq = x * x
    width = min(LANES, sq.shape[1])
    lane = lax.broadcasted_iota(jnp.int32, (1, width), 1)
    best = None
    for c in range(0, sq.shape[1], width):
        tile = sq[:, c:c + width]
        for g0 in range(0, width, group):
            part = tile if group == width else jnp.where((lane >= g0) & (lane < g0 + group), tile, 0.0)
            top = jnp.max(jnp.sum(part, axis=-1, keepdims=True), axis=0, keepdims=True)
            best = top if best is None else jnp.maximum(best, top)
    o_ref[0] = jnp.broadcast_to(best, o_ref.shape[1:])


def _key_norm_max(k, group):
    b, l, n = k.shape
    out = pl.pallas_call(
        functools.partial(_key_norm_kernel, group=group),
        out_shape=jax.ShapeDtypeStruct((b, 8, LANES), F32),
        grid=(b,),
        in_specs=[pl.BlockSpec((1, l, n), lambda b_: (b_, 0, 0))],
        out_specs=pl.BlockSpec((1, 8, LANES), lambda b_: (b_, 0, 0)),
        compiler_params=_params("parallel"),
        name="key_norm_max",
    )(k)
    return jnp.sqrt(jnp.max(out))


def _attn_kernel(*refs, mode, ctx, mla, sink, tq, rows, pp, group, bounded):
    it = iter(refs)
    q_ref, k_ref, v_ref = next(it), next(it), next(it)
    kc_ref, vc_ref = (next(it), next(it)) if ctx else (None, None)
    q2_ref, k2_ref = (next(it), next(it)) if mla else (None, None)
    kc2_ref = next(it) if (mla and ctx) else None
    sink_ref = next(it) if sink else None
    tbl_ref = next(it) if mode in ("na", "swa") else None
    o_ref = next(it)

    g = pl.program_id(1)
    t = pl.program_id(2)
    lane = lax.broadcasted_iota(jnp.int32, (1, LANES), 1)
    plans = [_attn_pair_plan(i, g, t, lane, q_ref, k_ref, v_ref, kc_ref, vc_ref, q2_ref, k2_ref, kc2_ref,
                             sink_ref, tbl_ref, mode=mode, ctx=ctx, mla=mla, sink=sink, tq=tq, rows=rows,
                             pp=pp, group=group)
             for i in range(pp)]

    def scores(i, j):
        kk, _, fix = plans[i][1][j]
        s = _dot(kk, plans[i][0])
        return s if fix is None else fix(s)

    n_chunks = len(plans[0][1])
    state = [None] * pp
    s_next = [scores(i, 0) for i in range(pp)]
    for j in range(n_chunks):
        for i in range(pp):
            s = s_next[i]
            if j + 1 < n_chunks:
                s_next[i] = scores(i, j + 1)
            vv = plans[i][1][j][1]
            state[i] = _softmax_step(state[i], s, vv, plans[i][2], bounded)
    for i in range(pp):
        m, den, acc = state[i]
        if sink:
            den = den + jnp.exp2(plans[i][2] if bounded else plans[i][2] - m)
        o = acc / den
        o_pair = jnp.concatenate([o[:HEAD_DIM, :tq], o[HEAD_DIM:, tq:]], axis=0)
        o_ref[0, :, i * LANES:(i + 1) * LANES] = o_pair.T.astype(o_ref.dtype)


def _softmax_step(state, s, vv, sk, bounded):
    def probs(m):
        pr = jnp.exp2(s if m is None else s - m)
        return pr.astype(BF16), jnp.sum(pr, axis=0, keepdims=True)

    if state is None:
        m = None
        if not bounded:
            m = jnp.max(s, axis=0, keepdims=True)
            m = m if sk is None else jnp.maximum(m, sk)
        pr, total = probs(m)
        return m, total, _dot_tn(vv, pr)
    m, den, acc = state
    if not bounded:
        m_new = jnp.maximum(m, jnp.max(s, axis=0, keepdims=True))
        alpha = jnp.exp2(m - m_new)
        pr, total = probs(m_new)
        return m_new, alpha * den + total, alpha * acc + _dot_tn(vv, pr)
    pr, total = probs(m)
    return m, den + total, acc + _dot_tn(vv, pr)


def _attn_pair_plan(i, g, t, lane, q_ref, k_ref, v_ref, kc_ref, vc_ref, q2_ref, k2_ref, kc2_ref, sink_ref,
                    tbl_ref, *, mode, ctx, mla, sink, tq, rows, pp, group):
    ql = slice(i * LANES, (i + 1) * LANES)
    kvi = i // group if pp >= group else 0
    kl = slice(kvi * LANES, (kvi + 1) * LANES)

    row = lax.broadcasted_iota(jnp.int32, (LANES, 1), 0)

    def stack(x, m0, m1):
        xt = x.astype(F32).T
        return jnp.concatenate([jnp.where(m0, xt, 0.0), jnp.where(m1, xt, 0.0)], axis=1).astype(BF16)

    qs = stack(q_ref[0, :, ql], row < HEAD_DIM, row >= HEAD_DIM)
    if mla:
        if pp == 1:
            q2, base = q2_ref[0], (g % 2) * (2 * MLA_ROPE)
        else:
            q2, base = q2_ref[0, :, (i // 2) * LANES:(i // 2 + 1) * LANES], (i % 2) * (2 * MLA_ROPE)
        m0 = (row >= base) & (row < base + MLA_ROPE)
        m1 = (row >= base + MLA_ROPE) & (row < base + 2 * MLA_ROPE)
        qs = jnp.concatenate([qs, stack(q2, m0, m1)], axis=0)

    def keys(ref, ref2, rows_):
        kk = ref[0, rows_, kl].astype(BF16)
        if mla:
            kk = jnp.concatenate([kk, ref2[0, rows_, :].astype(BF16)], axis=1)
        return kk

    chunks = []
    if ctx:
        lc = kc_ref.shape[1]
        kc_len = min(KEY_CHUNK, lc)
        for c in range(0, lc, kc_len):
            rows_ = slice(c, c + kc_len)
            chunks.append((keys(kc_ref, kc2_ref, rows_), vc_ref[0, rows_, kl].astype(BF16), None))
    if mode == "full":
        lk = k_ref.shape[1]
        kc_len = min(KEY_CHUNK, lk)
        for c in range(0, lk, kc_len):
            rows_ = slice(c, c + kc_len)
            chunks.append((keys(k_ref, k2_ref, rows_), v_ref[0, rows_, kl].astype(BF16), None))
    elif mode == "swa":
        win = tq + 2 * SWA_WINDOW
        ws = pl.multiple_of(jnp.clip(t * tq - SWA_WINDOW, 0, k_ref.shape[1] - win), SWA_WINDOW)

        def band(s):
            return s + tbl_ref[(t * tq - ws) // SWA_WINDOW]

        chunks.append((k_ref[0, pl.ds(ws, win), kl], v_ref[0, pl.ds(ws, win), kl], band))
    else:
        qrows = tq // GRID_W
        wrows = qrows + NA_WIN_ROWS - 1
        r0 = t * qrows
        ws_row = jnp.clip(r0 - NA_WIN_ROWS // 2, 0, rows - wrows)
        ws = pl.multiple_of(ws_row * GRID_W, GRID_W)

        def window_bias(s):
            e0 = ws_row - r0 + (NA_WIN_ROWS - 1) + qrows - 1
            out_rows = []
            for j in range(wrows):
                kr = ws_row + j
                blocks = []
                for hh in range(2):
                    for ii in range(qrows // 2):
                        ok = []
                        for qr in (r0 + 2 * ii, r0 + 2 * ii + 1):
                            rs = jnp.clip(qr - NA_WIN_ROWS // 2, 0, rows - NA_WIN_ROWS)
                            ok.append(((kr >= rs) & (kr < rs + NA_WIN_ROWS)).astype(jnp.int32))
                        valid = jnp.where(lane < GRID_W, ok[0], ok[1]) > 0
                        lo = hh * tq + ii * LANES
                        blk = (s[j * GRID_W:(j + 1) * GRID_W, lo:lo + LANES]
                               + tbl_ref[2 * i + hh, e0 + j - 2 * ii])
                        blocks.append(jnp.where(valid, blk, NEG_INF))
                out_rows.append(jnp.concatenate(blocks, axis=1))
            return jnp.concatenate(out_rows, axis=0)

        chunks.append((k_ref[0, pl.ds(ws, wrows * GRID_W), kl], v_ref[0, pl.ds(ws, wrows * GRID_W), kl],
                       window_bias))

    sk = None
    if sink:
        first = lax.broadcasted_iota(jnp.int32, (1, 2 * tq), 1) < tq
        head = 2 * (g * pp + i)
        sk = jnp.where(first, sink_ref[head], sink_ref[head + 1])
    return qs, chunks, sk


def _attention(q, k, v, *, mode="full", kc=None, vc=None, q2=None, k2=None, kc2=None, sink=None,
               na_table=None, bounded=False, pp=1, q_tile=Q_TILE):
    b, lq, nq = q.shape
    lk = k.shape[1]
    pairs = nq // LANES
    group = pairs // (k.shape[2] // LANES)
    tq = min(q_tile, lq)
    ctx, mla = kc is not None, q2 is not None
    assert pairs % pp == 0 and (pp % group == 0 or group % pp == 0)
    assert not mla or pp == 1 or pp % 2 == 0

    kv_lanes = LANES * max(1, pp // group)
    qmap = lambda b_, g, t: (b_, t, g)
    if pp >= group:
        kvmap = lambda b_, g, t: (b_, 0, g)
    else:
        kvmap = lambda b_, g, t: (b_, 0, (g * pp) // group)
    shared = lambda b_, g, t: (b_, 0, 0)
    args = [q, k, v]
    in_specs = [pl.BlockSpec((1, tq, pp * LANES), qmap),
                pl.BlockSpec((1, lk, kv_lanes), kvmap),
                pl.BlockSpec((1, lk, kv_lanes), kvmap)]
    if ctx:
        lc = kc.shape[1]
        args += [kc, vc]
        in_specs += [pl.BlockSpec((1, lc, kv_lanes), kvmap)] * 2
    if mla:
        args += [q2, k2]
        if pp == 1:
            in_specs += [pl.BlockSpec((1, tq, LANES), lambda b_, g, t: (b_, t, g // 2))]
        else:
            in_specs += [pl.BlockSpec((1, tq, pp * LANES // 2), qmap)]
        in_specs += [pl.BlockSpec((1, lk, LANES), shared)]
        if ctx:
            args += [kc2]
            in_specs += [pl.BlockSpec((1, kc2.shape[1], LANES), shared)]
    if sink is not None:
        args += [sink]
        in_specs += [pl.BlockSpec(memory_space=pltpu.SMEM)]
    if mode == "na":
        args += [na_table]
        in_specs += [pl.BlockSpec((2 * pp,) + na_table.shape[1:], lambda b_, g, t: (g, 0, 0, 0))]
    if mode == "swa":
        assert tq % SWA_WINDOW == 0
        mask = _swa_band_masks(tq)
        args += [mask]
        in_specs += [_resident(mask.shape)]
    return pl.pallas_call(
        functools.partial(_attn_kernel, mode=mode, ctx=ctx, mla=mla, sink=sink is not None, tq=tq,
                          rows=lq // GRID_W, pp=pp, group=group, bounded=bounded),
        out_shape=jax.ShapeDtypeStruct((b, lq, nq), BF16),
        grid=(b, pairs // pp, lq // tq),
        in_specs=in_specs,
        out_specs=pl.BlockSpec((1, tq, pp * LANES), qmap),
        compiler_params=_params("parallel", "parallel", "parallel"),
        name="attention_" + mode,
    )(*args)


def _attention_guarded(score_bound, *args, **kwargs):
    return lax.cond(score_bound * BOUND_SLACK <= MAX_SCORE_BOUND,
                    lambda: _attention(*args, bounded=True, **kwargs),
                    lambda: _attention(*args, **kwargs))


def _out_ffn_kernel(x_ref, a_ref, mod_ref, g_ref, wo_ref, wg_ref, wu_ref, wd_ref, o_ref, *, chunks):
    mod = lambda r: mod_ref[0, r:r + 1, :]
    x1 = x_ref[0] + mod(2) * _dot(a_ref[0], wo_ref[...])
    h = _norm_mod(x1, g_ref[...], mod(3), mod(4)).astype(BF16)
    hidden = wg_ref.shape[1]
    step = hidden // chunks
    gate_up = lambda c: (_dot(h, wg_ref[:, c:c + step]), _dot(h, wu_ref[:, c:c + step]))
    y = None
    nxt = gate_up(0)
    for c in range(0, hidden, step):
        gate, up = nxt
        if c + step < hidden:
            nxt = gate_up(c + step)
        act = (gate * jax.nn.sigmoid(gate) * up).astype(BF16)
        part = _dot(act, wd_ref[c:c + step, :])
        y = part if y is None else y + part
    o_ref[0] = x1 + mod(5) * y


def _out_ffn(x, attn, mod, g2, w_o, w_gate, w_up, w_down, layer):
    b, l, d = x.shape
    tl = min(TOKEN_TILE, l)
    tok = lambda n: pl.BlockSpec((1, tl, n), lambda b_, i: (b_, i, 0))
    of_layer = lambda w: pl.BlockSpec((None,) + w.shape[1:], lambda b_, i: (layer, 0, 0),
                                      pipeline_mode=pl.Buffered(1))
    return pl.pallas_call(
        functools.partial(_out_ffn_kernel, chunks=11),
        out_shape=jax.ShapeDtypeStruct((b, l, d), F32),
        grid=(b, l // tl),
        in_specs=[tok(d), tok(attn.shape[2]), _mod_spec(mod), _resident((1, d)),
                  _resident(w_o.shape), of_layer(w_gate), of_layer(w_up), of_layer(w_down)],
        out_specs=tok(d),
        compiler_params=_params("parallel", "parallel"),
        name="out_ffn",
    )(x, attn, mod, g2.reshape(1, d), w_o, w_gate, w_up, w_down)


def _swa_band_masks(tq):
    i = np.arange(tq + 2 * SWA_WINDOW)[None, :, None]
    j = (np.arange(2 * tq) % tq)[None, None, :]
    delta = (np.arange(3) * SWA_WINDOW)[:, None, None]
    return jnp.asarray(np.where(np.abs(i - delta - j) <= SWA_WINDOW, 0.0, NEG_INF), F32)


def _max_abs(a):
    return jnp.max(jnp.abs(a))


def _block_diag_ones(n, group):
    idx = np.arange(n) // group
    return jnp.asarray(idx[:, None] == idx[None, :], BF16)


def _tile_gain(g, scale=1.0):
    return jnp.tile(g * scale, MXU_COLS // g.shape[0]).reshape(1, MXU_COLS)


def _rope_tables(n_tokens, r):
    quarter = r // 4
    n_rows = n_tokens // GRID_W
    inv_freq = ROPE_THETA ** (-jnp.arange(quarter, dtype=F32) / quarter)
    ang = jnp.arange(max(n_rows, GRID_W), dtype=F32)[:, None] * inv_freq[None, :]
    by_row = lambda a: jnp.broadcast_to(a[:n_rows, None, :], (n_rows, GRID_W, quarter)).reshape(n_tokens, quarter)
    by_col = lambda a: jnp.broadcast_to(a[None, :GRID_W, :], (n_rows, GRID_W, quarter)).reshape(n_tokens, quarter)
    cos, sin = jnp.cos(ang), jnp.sin(ang)
    cos_head = jnp.concatenate([by_row(cos), by_row(cos), by_col(cos), by_col(cos)], axis=-1)
    sin_head = jnp.concatenate([-by_row(sin), by_row(sin), -by_col(sin), by_col(sin)], axis=-1)
    return jnp.tile(cos_head, (1, MXU_COLS // r)), jnp.tile(sin_head, (1, MXU_COLS // r))


_GQA_HEADS = np.array([8 * (p // 4) + 4 * e + (p % 4) for p in range(N_HEADS // 2) for e in range(2)])
_GQA_COLS = (_GQA_HEADS[:, None] * HEAD_DIM + np.arange(HEAD_DIM)[None, :]).reshape(-1)


def _na_bias_table(rpb, q_rows):
    n_heads, _, n_dc = rpb.shape
    qc = np.arange(LANES)[None, :] % GRID_W
    kc = np.arange(GRID_W)[:, None]
    start = np.clip(qc - NA_WIN_COLS // 2, 0, GRID_W - NA_WIN_COLS)
    valid = (kc >= start) & (kc < start + NA_WIN_COLS)
    dcol = np.clip(kc - qc + NA_WIN_COLS - 1, 0, n_dc - 1) + n_dc * (np.arange(LANES)[None, :] // GRID_W)
    onehot = (np.arange(2 * n_dc)[:, None] == dcol.reshape(1, -1)).astype(np.float32)
    n_e = 2 * q_rows + 2 * NA_WIN_ROWS - 2
    rp = jnp.pad(rpb, ((0, 0), (q_rows, q_rows), (0, 0)))
    pairs = jnp.concatenate([rp[:, 1:1 + n_e], rp[:, 0:n_e]], axis=-1)
    t = jnp.dot(pairs.reshape(n_heads * n_e, 2 * n_dc), onehot, precision=lax.Precision.HIGHEST)
    return jnp.where(valid[None, None], t.reshape(n_heads, n_e, GRID_W, LANES), NEG_INF)


def kernel(x_prompt, x_sample, cache_na_k, cache_na_v, cache_swa_k, cache_swa_v, cache_mla_ckv, cache_mla_kpe, cache_gqa_k, cache_gqa_v, c, c_ctx, norm1_g, norm2_g, w_mod, b_mod, na_w_qkv, na_q_g, na_k_g, na_rpb, na_w_o, swa_w_qkv, swa_q_g, swa_k_g, swa_sink, swa_w_o, mla_w_dq, mla_q_lora_g, mla_w_uq, mla_q_nope_g, mla_q_pe_g, mla_w_dkv, mla_kv_lora_g, mla_k_pe_g, mla_w_ukv, mla_k_nope_g, mla_w_o, gqa_w_qkv, gqa_q_g, gqa_k_g, gqa_w_o, ffn_w_gate, ffn_w_up, ffn_w_down):
    depth, d = norm1_g.shape
    bp, lp, _ = x_prompt.shape
    bs, ls, _ = x_sample.shape
    hq, hkv = N_HEADS * HEAD_DIM, N_KV_HEADS * HEAD_DIM
    scale = HEAD_DIM ** -0.5 * LOG2E

    cond = jnp.concatenate([c_ctx[None, :], c, jnp.zeros((16 - 1 - bs, d), F32)], axis=0)
    mods = _adaln(cond, w_mod, b_mod).reshape(depth, 16, 6, d)
    rope64 = _rope_tables(ls, HEAD_DIM)
    rope32 = _rope_tables(ls, MLA_ROPE)
    flat = lambda a, r: a[:, r].reshape(a.shape[0], a.shape[2], -1)

    wg, wu, wd = ffn_w_gate.astype(BF16), ffn_w_up.astype(BF16), ffn_w_down.astype(BF16)
    fold_n = max(1, TOKEN_TILE // lp)
    fold_n = fold_n if bp % fold_n == 0 else 1
    fold = lambda a: a.reshape(a.shape[0] // fold_n, fold_n * a.shape[1], a.shape[2])
    unfold = lambda a: lax.optimization_barrier(a.reshape(bp, lp, a.shape[2]))
    xp, xs = fold(x_prompt), x_sample
    outs = {}
    for li in range(depth):
        kind, r = li % 4, li // 4
        mp, ms = mods[li, 0:1], mods[li, 1:1 + bs]
        g1 = norm1_g[li]
        if kind in (0, 1, 3):
            w_qkv, q_g, k_g, w_o = ((na_w_qkv, na_q_g, na_k_g, na_w_o), (swa_w_qkv, swa_q_g, swa_k_g, swa_w_o),
                                    None, (gqa_w_qkv, gqa_q_g, gqa_k_g, gqa_w_o))[kind]
            w_qkv, w_o = w_qkv[r], w_o[r]
            nk = hq if kind == 0 else hkv
            if kind != 0:
                w_qkv = jnp.concatenate([w_qkv[:, :hq][:, _GQA_COLS], w_qkv[:, hq:]], axis=1)
                w_o = w_o[_GQA_COLS, :]
            w_qkv, w_o = w_qkv.astype(BF16), w_o.astype(BF16)
            qp, kp, vp = map(unfold, _qkv_project(xp, g1, mp, w_qkv, q_g[r] * scale, k_g[r], hq, nk, F32))
            qs, ks, vs = _qkv_project(xs, g1, ms, w_qkv, q_g[r] * scale, k_g[r], hq, nk, BF16,
                                      rope_tables=None if kind == 0 else rope64)
            kc, vc = [flat(a, r) for a in ((cache_na_k, cache_na_v), (cache_swa_k, cache_swa_v), None,
                                           (cache_gqa_k, cache_gqa_v))[kind]]
            q_norm = HEAD_DIM ** 0.5 * _max_abs(q_g[r] * scale)
            qk_bound = q_norm * HEAD_DIM ** 0.5 * _max_abs(k_g[r])
            ctx_bound = q_norm * _key_norm_max(kc, HEAD_DIM)
            if kind == 0:
                rpb = na_rpb[r] * LOG2E
                ap = _attention_guarded(qk_bound, qp, kp, vp, pp=PROMPT_PAIRS)
                as_ = _attention_guarded(jnp.maximum(qk_bound + _max_abs(rpb), ctx_bound), qs, ks, vs, mode="na",
                                         kc=kc, vc=vc, na_table=_na_bias_table(rpb, WINDOW_Q_TILE // GRID_W),
                                         pp=WINDOW_PAIRS, q_tile=WINDOW_Q_TILE)
                outs["na_k"], outs["na_v"] = kp, vp
            elif kind == 1:
                sink = swa_sink[r][_GQA_HEADS] * LOG2E
                qk_bound = jnp.maximum(qk_bound, _max_abs(sink))
                ap = _attention_guarded(qk_bound, qp, kp, vp, sink=sink, pp=PROMPT_PAIRS)
                as_ = _attention_guarded(jnp.maximum(qk_bound, ctx_bound), qs, ks, vs, mode="swa", kc=kc, vc=vc,
                                         sink=sink, pp=WINDOW_PAIRS, q_tile=WINDOW_Q_TILE)
                outs["swa_k"], outs["swa_v"] = kp, vp
            else:
                ap = _attention_guarded(qk_bound, qp, kp, vp, pp=PROMPT_PAIRS)
                as_ = _attention_guarded(jnp.maximum(qk_bound, ctx_bound), qs, ks, vs, kc=kc, vc=vc,
                                         pp=LATENT_PAIRS)
                outs["gqa_k"], outs["gqa_v"] = kp, vp
        else:
            dk = MLA_NOPE + MLA_ROPE
            w_uq = mla_w_uq[r].reshape(-1, N_HEADS, dk)
            w_uq = jnp.concatenate([w_uq[:, :, :MLA_NOPE].reshape(-1, N_HEADS * MLA_NOPE),
                                    w_uq[:, :, MLA_NOPE:].reshape(-1, N_HEADS * MLA_ROPE)], axis=1).astype(BF16)
            w_ukv = mla_w_ukv[r].reshape(-1, N_HEADS, MLA_NOPE + MLA_V)
            w_ukv = jnp.concatenate([w_ukv[:, :, :MLA_NOPE].reshape(-1, N_HEADS * MLA_NOPE),
                                     w_ukv[:, :, MLA_NOPE:].reshape(-1, N_HEADS * MLA_V)], axis=1).astype(BF16)
            w_dkv = jnp.pad(mla_w_dkv[r], ((0, 0), (0, 2 * LANES - MLA_KV_LORA - MLA_ROPE))).astype(BF16)
            w_dq, w_o = mla_w_dq[r].astype(BF16), mla_w_o[r].astype(BF16)
            qscale = dk ** -0.5 * LOG2E
            proj = functools.partial(
                _mla_project, w_dq=w_dq, q_lora_g=mla_q_lora_g[r], w_uq=w_uq, q_nope_g=mla_q_nope_g[r] * qscale,
                q_pe_g=mla_q_pe_g[r] * qscale, w_dkv=w_dkv, kv_lora_g=mla_kv_lora_g[r], k_pe_g=mla_k_pe_g[r],
                w_ukv=w_ukv, k_nope_g=mla_k_nope_g[r])
            qn_p, qpe_p, ckv_p, kpe_p, kn_p, v_p, kpt_p = map(unfold, proj(xp, g1, mp))
            qn_s, qpe_s, kn_s, v_s, kpt_s = proj(xs, g1, ms, rope_tables=rope32, emit_compressed=False)
            kn_c, v_c, kpt_c = _mla_expand(cache_mla_ckv[:, r], cache_mla_kpe[:, r], w_ukv, mla_k_nope_g[r])
            q_norm = jnp.sqrt(MLA_NOPE * _max_abs(mla_q_nope_g[r] * qscale) ** 2
                              + MLA_ROPE * _max_abs(mla_q_pe_g[r] * qscale) ** 2)
            k_norm = jnp.sqrt(MLA_NOPE * _max_abs(mla_k_nope_g[r]) ** 2 + MLA_ROPE * _max_abs(mla_k_pe_g[r]) ** 2)
            ctx_k_norm = jnp.sqrt(MLA_NOPE * _max_abs(mla_k_nope_g[r]) ** 2
                                  + _key_norm_max(cache_mla_kpe[:, r], MLA_ROPE) ** 2)
            ap = _attention_guarded(q_norm * k_norm, qn_p, kn_p, v_p, q2=qpe_p, k2=kpt_p, pp=PROMPT_PAIRS)
            as_ = _attention_guarded(q_norm * jnp.maximum(k_norm, ctx_k_norm), qn_s, kn_s, v_s, kc=kn_c, vc=v_c,
                                     q2=qpe_s, k2=kpt_s, kc2=kpt_c, pp=LATENT_PAIRS)
            outs["mla_ckv"], outs["mla_kpe"] = ckv_p, kpe_p[:, :, :MLA_ROPE]
        xp = _out_ffn(xp, fold(ap), mp, norm2_g[li], w_o, wg, wu, wd, li)
        xs = _out_ffn(xs, as_, ms, norm2_g[li], w_o, wg, wu, wd, li)

    heads = lambda a, h: a.reshape(bp, 1, lp, h, HEAD_DIM)
    return (unfold(xp), xs, heads(outs["na_k"], N_HEADS), heads(outs["na_v"], N_HEADS),
            heads(outs["swa_k"], N_KV_HEADS), heads(outs["swa_v"], N_KV_HEADS),
            outs["mla_ckv"][:, None], outs["mla_kpe"][:, None],
            heads(outs["gqa_k"], N_KV_HEADS), heads(outs["gqa_v"], N_KV_HEADS))
```
